```python
import math
import jax, jax.numpy as jnp
from jax import lax
import numpy as np

D_MODEL = 2048
BATCH = 4
SEQ = 2048
DEPTH = 2
DEC_BATCH = 128
DEC_SEQ = 8
PAST_LEN = 8192
PAGE_SIZE = 128

N_EVEN = (DEPTH + 1) // 2
N_ODD = DEPTH // 2
HEAD_DIM = 128
MOBA_HEADS = 8
MOBA_KV_HEADS = 2
MOBA_BLOCK = 256
MOBA_TOPK = 3
MOBA_QCHUNK = 32
DIFF_HEADS = 8
DIFF_KV_HEADS = 2
DIFF_DIM = 64
ATTN_QCHUNK = 128
MLA_HEADS = 16
Q_LORA = 512
KV_LORA = 512
NOPE_DIM = 128
ROPE_DIM = 64
V_DIM = 128
D_FF = 5632
CONV_W = 3
ROPE_THETA = 10000.0
EPS = 1e-6

AB_COLS = (MOBA_HEADS * HEAD_DIM, MOBA_KV_HEADS * HEAD_DIM, MOBA_KV_HEADS * HEAD_DIM,
           DIFF_HEADS * 2 * DIFF_DIM, DIFF_KV_HEADS * 2 * DIFF_DIM, DIFF_KV_HEADS * 2 * DIFF_DIM)
AB_SPLITS = tuple(sum(AB_COLS[:i + 1]) for i in range(len(AB_COLS) - 1))
AB_IN = sum(AB_COLS)
AB_OUT = MOBA_HEADS * HEAD_DIM + DIFF_HEADS * 2 * DIFF_DIM
STATE_NAMES = ('moba_k', 'moba_v', 'diff_k', 'diff_v', 'mla_ckv', 'mla_krope', 'ffn_conv')

kernel_name = 'moba_diff_mla_convffn_step'


def rms_norm(x, g):
    xf = x.astype(jnp.float32)
    y = xf * lax.rsqrt(jnp.mean(xf * xf, axis=-1, keepdims=True) + EPS)
    return (y * g.astype(jnp.float32)).astype(x.dtype)


def rope(x, pos):
    d = x.shape[-1]
    inv = 1.0 / (ROPE_THETA ** (jnp.arange(0, d, 2, dtype=jnp.float32) / d))
    ang = pos.astype(jnp.float32)[:, None] * inv
    ang = ang.reshape(ang.shape[:1] + (1,) * (x.ndim - 3) + ang.shape[1:])
    cos, sin = jnp.cos(ang), jnp.sin(ang)
    x1, x2 = jnp.split(x.astype(jnp.float32), 2, axis=-1)
    return jnp.concatenate([x1 * cos - x2 * sin, x2 * cos + x1 * sin], axis=-1).astype(x.dtype)


def map_sequences(core, qs, kv_new, pools, layer, page_table):
    if pools is None:
        return lax.map(lambda a: core(a[0], a[1]), (qs, kv_new))

    def one(a):
        q_i, kv_i, pages = a
        kv_full = tuple(
            jnp.concatenate([pool[layer, pages].reshape((-1,) + pool.shape[3:]), new], axis=0)
            for pool, new in zip(pools, kv_i))
        return core(q_i, kv_full)

    return lax.map(one, (qs, kv_new, page_table))


def moba_core(q, k, v, q0):
    T, L = q.shape[0], k.shape[0]
    G = MOBA_HEADS // MOBA_KV_HEADS
    nb = -(-L // MOBA_BLOCK)
    pad = ((0, nb * MOBA_BLOCK - L), (0, 0), (0, 0))
    kb = jnp.pad(k, pad).reshape(nb, MOBA_BLOCK, MOBA_KV_HEADS, HEAD_DIM).transpose(0, 2, 1, 3)
    vb = jnp.pad(v, pad).reshape(nb, MOBA_BLOCK, MOBA_KV_HEADS, HEAD_DIM).transpose(0, 2, 1, 3)
    kmean = jnp.mean(kb.astype(jnp.float32), axis=2)
    n_sel = max(1, min(MOBA_TOPK, nb - 1))
    qc = MOBA_QCHUNK if T % MOBA_QCHUNK == 0 else T
    qs = q.reshape(T // qc, qc, MOBA_KV_HEADS, G, HEAD_DIM)
    pos = (q0 + jnp.arange(T, dtype=jnp.int32)).reshape(T // qc, qc)
    blk = jnp.arange(nb, dtype=jnp.int32)
    row = jnp.arange(MOBA_BLOCK, dtype=jnp.int32)
    kvh = jnp.arange(MOBA_KV_HEADS, dtype=jnp.int32)[None, :, None, None]
    scale = HEAD_DIM ** -0.5

    def chunk(args):
        qch, p = args
        own = (p // MOBA_BLOCK)[:, None, None, None]
        gate = jnp.einsum('tkgd,nkd->tkgn', qch.astype(jnp.float32), kmean)
        gate = jnp.where(blk < own, gate, -jnp.inf)
        _, sel = lax.top_k(gate, n_sel)
        valid = sel < own
        blocks = jnp.concatenate([sel, jnp.broadcast_to(own, sel.shape[:-1] + (1,))], axis=-1)
        kg = kb[blocks, kvh]
        vg = vb[blocks, kvh]
        s = jnp.einsum('tkgd,tkgbsd->tkgbs', qch, kg, preferred_element_type=jnp.float32) * scale
        kpos = blocks[..., None] * MOBA_BLOCK + row
        slot_ok = jnp.concatenate([valid, jnp.ones_like(valid[..., :1])], axis=-1)
        ok = slot_ok[..., None] & (kpos <= p[:, None, None, None, None])
        s = jnp.where(ok, s, -jnp.inf)
        w = jax.nn.softmax(s.reshape(s.shape[:3] + (-1,)), axis=-1).reshape(s.shape)
        return jnp.einsum('tkgbs,tkgbsd->tkgd', w.astype(v.dtype), vg)

    o = lax.map(chunk, (qs, pos))
    return o.reshape(T, MOBA_HEADS * HEAD_DIM)


def diff_core(q, k, v, q0, lam, lam_init, sub_g):
    T, L = q.shape[0], k.shape[0]
    G = DIFF_HEADS // DIFF_KV_HEADS
    qc = ATTN_QCHUNK if T % ATTN_QCHUNK == 0 else T
    qs = q.reshape(T // qc, qc, DIFF_KV_HEADS, G, 2, DIFF_DIM)
    pos = (q0 + jnp.arange(T, dtype=jnp.int32)).reshape(T // qc, qc)
    kpos = jnp.arange(L, dtype=jnp.int32)
    scale = DIFF_DIM ** -0.5

    def chunk(args):
        qch, p = args
        s = jnp.einsum('tkgmd,lkmd->kgmtl', qch, k, preferred_element_type=jnp.float32) * scale
        s = jnp.where(kpos <= p[:, None], s, -jnp.inf)
        a = jax.nn.softmax(s, axis=-1)
        a = a[:, :, 0] - lam * a[:, :, 1]
        return jnp.einsum('kgtl,lke->tkge', a.astype(v.dtype), v)

    o = lax.map(chunk, (qs, pos)).reshape(T, DIFF_HEADS, 2 * DIFF_DIM)
    o = rms_norm(o, sub_g) * (1.0 - lam_init)
    return o.reshape(T, DIFF_HEADS * 2 * DIFF_DIM)


def mla_core(q_lat, q_rope, ckv, krope, q0):
    T, L = q_lat.shape[0], ckv.shape[0]
    qc = ATTN_QCHUNK if T % ATTN_QCHUNK == 0 else T
    pos = (q0 + jnp.arange(T, dtype=jnp.int32)).reshape(T // qc, qc)
    kpos = jnp.arange(L, dtype=jnp.int32)
    scale = (NOPE_DIM + ROPE_DIM) ** -0.5

    def chunk(args):
        ql, qr, p = args
        s = (jnp.einsum('thc,lc->htl', ql, ckv, preferred_element_type=jnp.float32)
             + jnp.einsum('thr,lr->htl', qr, krope, preferred_element_type=jnp.float32)) * scale
        s = jnp.where(kpos[None, None, :] <= p[None, :, None], s, -jnp.inf)
        w = jax.nn.softmax(s, axis=-1)
        return jnp.einsum('htl,lc->thc', w.astype(ckv.dtype), ckv)

    o = lax.map(chunk, (q_lat.reshape(T // qc, qc, MLA_HEADS, KV_LORA),
                        q_rope.reshape(T // qc, qc, MLA_HEADS, ROPE_DIM), pos))
    return o.reshape(T, MLA_HEADS, KV_LORA)


def mixer_ab(h, pos0, w_in, w_out, lam_vec, sub_g, lam_init, moba_pools, diff_pools, layer, page_table):
    B, T, _ = h.shape
    pos = pos0 + jnp.arange(T, dtype=jnp.int32)
    mq, mk, mv, dq, dk, dv = jnp.split(h @ w_in, AB_SPLITS, axis=-1)
    mq = rope(mq.reshape(B, T, MOBA_HEADS, HEAD_DIM), pos)
    mk = rope(mk.reshape(B, T, MOBA_KV_HEADS, HEAD_DIM), pos)
    mv = mv.reshape(B, T, MOBA_KV_HEADS, HEAD_DIM)
    dq = rope(dq.reshape(B, T, DIFF_HEADS, 2, DIFF_DIM), pos)
    dk = rope(dk.reshape(B, T, DIFF_KV_HEADS, 2, DIFF_DIM), pos)
    dv = dv.reshape(B, T, DIFF_KV_HEADS, 2 * DIFF_DIM)
    lv = lam_vec.astype(jnp.float32)
    lam = jnp.exp(jnp.dot(lv[0], lv[1])) - jnp.exp(jnp.dot(lv[2], lv[3])) + lam_init
    o_m = map_sequences(lambda qs, kv: moba_core(qs[0], kv[0], kv[1], pos0),
                        (mq,), (mk, mv), moba_pools, layer, page_table)
    o_d = map_sequences(lambda qs, kv: diff_core(qs[0], kv[0], kv[1], pos0, lam, lam_init, sub_g),
                        (dq,), (dk, dv), diff_pools, layer, page_table)
    y = jnp.concatenate([o_m, o_d], axis=-1) @ w_out
    return y, (mk, mv, dk, dv)


def mixer_mla(h, pos0, w_dq, g_q, w_uq, w_dkv, g_kv, w_uk, w_uv, w_o, pools, layer, page_table):
    B, T, _ = h.shape
    pos = pos0 + jnp.arange(T, dtype=jnp.int32)
    cq = rms_norm(h @ w_dq, g_q)
    q = (cq @ w_uq).reshape(B, T, MLA_HEADS, NOPE_DIM + ROPE_DIM)
    q_nope, q_rope = q[..., :NOPE_DIM], rope(q[..., NOPE_DIM:], pos)
    kv = h @ w_dkv
    ckv = rms_norm(kv[..., :KV_LORA], g_kv)
    krope = rope(kv[..., KV_LORA:], pos)
    q_lat = jnp.einsum('bthn,chn->bthc', q_nope, w_uk)
    o_lat = map_sequences(lambda qs, kvs: mla_core(qs[0], qs[1], kvs[0], kvs[1], pos0),
                          (q_lat, q_rope), (ckv, krope), pools, layer, page_table)
    o = jnp.einsum('bthc,chv->bthv', o_lat, w_uv).reshape(B, T, MLA_HEADS * V_DIM)
    return o @ w_o, (ckv, krope)


def conv_ffn(h, past_rows, w_gate, w_up, conv_w, conv_b, w_down):
    B, T, _ = h.shape
    g = h @ w_gate
    u = h @ w_up
    if past_rows is None:
        past_rows = jnp.zeros((B, CONV_W - 1, D_FF), g.dtype)
    gp = jnp.concatenate([past_rows.astype(g.dtype), g], axis=1)
    gc = conv_b
    for j in range(CONV_W):
        gc = gc + conv_w[j] * gp[:, j:j + T]
    y = (jax.nn.gelu(gc, approximate=True) * u) @ w_down
    return y, gp[:, T:]


def trunk(x, pos0, prm, past):
    rows = {name: [] for name in STATE_NAMES}
    pt = None if past is None else past['page_table']
    for layer in range(DEPTH):
        i = layer // 2
        g = prm['norm_gains'][layer]
        h = rms_norm(x, g[0])
        if layer % 2 == 0:
            lam_init = 0.8 - 0.6 * math.exp(-0.3 * layer)
            mix, (mk, mv, dk, dv) = mixer_ab(
                h, pos0, prm['w_in_ab'][i], prm['w_out_ab'][i], prm['diff_lambda'][i], prm['diff_subln'][i],
                lam_init, None if past is None else past['moba'], None if past is None else past['diff'], i, pt)
            rows['moba_k'].append(mk)
            rows['moba_v'].append(mv)
            rows['diff_k'].append(dk)
            rows['diff_v'].append(dv)
        else:
            mix, (ckv, krope) = mixer_mla(
                h, pos0, prm['mla_w_dq'][i], prm['mla_g_q'][i], prm['mla_w_uq'][i], prm['mla_w_dkv'][i],
                prm['mla_g_kv'][i], prm['mla_w_uk'][i], prm['mla_w_uv'][i], prm['mla_w_o'][i],
                None if past is None else past['mla'], i, pt)
            rows['mla_ckv'].append(ckv)
            rows['mla_krope'].append(krope)
        x = x + rms_norm(mix, g[1])
        f, conv_rows = conv_ffn(rms_norm(x, g[2]), None if past is None else past['conv'][layer],
                                prm['ffn_w_gate'][layer], prm['ffn_w_up'][layer], prm['ffn_conv_w'][layer],
                                prm['ffn_conv_b'][layer], prm['ffn_w_down'][layer])
        rows['ffn_conv'].append(conv_rows)
        x = x + rms_norm(f, g[3])
    return x, {name: jnp.stack(v) for name, v in rows.items()}


def setup_inputs(seed: int = 0) -> dict:
    key = jax.random.key(seed)
    ks = iter(jax.random.split(key, 32))
    f32 = jnp.float32

    def nrm(shape, scale=1.0):
        return jax.random.normal(next(ks), shape, f32) * scale

    def gain(shape):
        return 1.0 + nrm(shape, 0.05)

    n_pages = PAST_LEN // PAGE_SIZE
    used = DEC_BATCH * n_pages
    n_phys = used + (used + 3) // 4
    return {
        'x_prompt': nrm((BATCH, SEQ, D_MODEL)),
        'x_sample': nrm((DEC_BATCH, DEC_SEQ, D_MODEL)),
        'cache_moba_k': nrm((N_EVEN, n_phys, PAGE_SIZE, MOBA_KV_HEADS, HEAD_DIM)),
        'cache_moba_v': nrm((N_EVEN, n_phys, PAGE_SIZE, MOBA_KV_HEADS, HEAD_DIM)),
        'cache_diff_k': nrm((N_EVEN, n_phys, PAGE_SIZE, DIFF_KV_HEADS, 2, DIFF_DIM)),
        'cache_diff_v': nrm((N_EVEN, n_phys, PAGE_SIZE, DIFF_KV_HEADS, 2 * DIFF_DIM)),
        'cache_mla_ckv': nrm((N_ODD, n_phys, PAGE_SIZE, KV_LORA)),
        'cache_mla_krope': nrm((N_ODD, n_phys, PAGE_SIZE, ROPE_DIM)),
        'state_ffn_conv': nrm((DEPTH, DEC_BATCH, CONV_W - 1, D_FF)),
        'page_table': jax.random.permutation(next(ks), n_phys)[:used].reshape(DEC_BATCH, n_pages).astype(jnp.int32),
        'norm_gains': gain((DEPTH, 4, D_MODEL)),
        'w_in_ab': nrm((N_EVEN, D_MODEL, AB_IN), D_MODEL ** -0.5),
        'w_out_ab': nrm((N_EVEN, AB_OUT, D_MODEL), AB_OUT ** -0.5),
        'diff_lambda': nrm((N_EVEN, 4, DIFF_DIM), 0.1),
        'diff_subln': gain((N_EVEN, 2 * DIFF_DIM)),
        'mla_w_dq': nrm((N_ODD, D_MODEL, Q_LORA), D_MODEL ** -0.5),
        'mla_g_q': gain((N_ODD, Q_LORA)),
        'mla_w_uq': nrm((N_ODD, Q_LORA, MLA_HEADS * (NOPE_DIM + ROPE_DIM)), Q_LORA ** -0.5),
        'mla_w_dkv': nrm((N_ODD, D_MODEL, KV_LORA + ROPE_DIM), D_MODEL ** -0.5),
        'mla_g_kv': gain((N_ODD, KV_LORA)),
        'mla_w_uk': nrm((N_ODD, KV_LORA, MLA_HEADS, NOPE_DIM), KV_LORA ** -0.5),
        'mla_w_uv': nrm((N_ODD, KV_LORA, MLA_HEADS, V_DIM), KV_LORA ** -0.5),
        'mla_w_o': nrm((N_ODD, MLA_HEADS * V_DIM, D_MODEL), (MLA_HEADS * V_DIM) ** -0.5),
        'ffn_w_gate': nrm((DEPTH, D_MODEL, D_FF), D_MODEL ** -0.5),
        'ffn_w_up': nrm((DEPTH, D_MODEL, D_FF), D_MODEL ** -0.5),
        'ffn_conv_w': nrm((DEPTH, CONV_W, D_FF), CONV_W ** -0.5),
        'ffn_conv_b': nrm((DEPTH, D_FF), 0.01),
        'ffn_w_down': nrm((DEPTH, D_FF, D_MODEL), D_FF ** -0.5),
    }


def reference(x_prompt, x_sample, cache_moba_k, cache_moba_v, cache_diff_k, cache_diff_v, cache_mla_ckv,
              cache_mla_krope, state_ffn_conv, page_table, norm_gains, w_in_ab, w_out_ab, diff_lambda, diff_subln,
              mla_w_dq, mla_g_q, mla_w_uq, mla_w_dkv, mla_g_kv, mla_w_uk, mla_w_uv, mla_w_o,
              ffn_w_gate, ffn_w_up, ffn_conv_w, ffn_conv_b, ffn_w_down):
    prm = {
        'norm_gains': norm_gains, 'w_in_ab': w_in_ab, 'w_out_ab': w_out_ab, 'diff_lambda': diff_lambda,
        'diff_subln': diff_subln, 'mla_w_dq': mla_w_dq, 'mla_g_q': mla_g_q, 'mla_w_uq': mla_w_uq,
        'mla_w_dkv': mla_w_dkv, 'mla_g_kv': mla_g_kv, 'mla_w_uk': mla_w_uk, 'mla_w_uv': mla_w_uv,
        'mla_w_o': mla_w_o, 'ffn_w_gate': ffn_w_gate, 'ffn_w_up': ffn_w_up, 'ffn_conv_w': ffn_conv_w,
        'ffn_conv_b': ffn_conv_b, 'ffn_w_down': ffn_w_down,
    }
    past = {
        'moba': (cache_moba_k, cache_moba_v), 'diff': (cache_diff_k, cache_diff_v),
        'mla': (cache_mla_ckv, cache_mla_krope), 'conv': state_ffn_conv, 'page_table': page_table,
    }
    y_prompt, new_p = trunk(x_prompt, 0, prm, None)
    y_sample, new_s = trunk(x_sample, PAST_LEN, prm, past)
    return (y_prompt, y_sample,
            new_p['moba_k'], new_p['moba_v'], new_p['diff_k'], new_p['diff_v'],
            new_p['mla_ckv'], new_p['mla_krope'], new_p['ffn_conv'],
            new_s['moba_k'], new_s['moba_v'], new_s['diff_k'], new_s['diff_v'],
            new_s['mla_ckv'], new_s['mla_krope'], new_s['ffn_conv'])
```

```python
import functools
import math

import jax
import jax.numpy as jnp
from jax import lax
from jax.experimental import pallas as pl
from jax.experimental.pallas import tpu as pltpu

D_MODEL = 2048
PAST_LEN = 8192
PAGE_SIZE = 128
HEAD_DIM = 128
MOBA_HEADS = 8
MOBA_KV_HEADS = 2
MOBA_BLOCK = 256
MOBA_TOPK = 3
DIFF_HEADS = 8
DIFF_KV_HEADS = 2
DIFF_DIM = 64
MLA_HEADS = 16
Q_LORA = 512
KV_LORA = 512
NOPE_DIM = 128
ROPE_DIM = 64
V_DIM = 128
D_FF = 5632
CONV_W = 3
ROPE_THETA = 10000.0
EPS = 1e-6

LANES = 128
SUBLANES = 8
VMEM_LIMIT = 56 * 1024 * 1024
NEG_INF = float("-inf")
BF16 = jnp.bfloat16
F32 = jnp.float32


def _cparams(sem):
    return pltpu.CompilerParams(dimension_semantics=sem, vmem_limit_bytes=VMEM_LIMIT)


def _rms(x, g):
    return x * lax.rsqrt(jnp.mean(x * x, axis=-1, keepdims=True) + EPS) * g


def _dot(a, b):
    return jnp.dot(a, b, preferred_element_type=F32)


def _dot_nt(a, b, precision=None):
    return lax.dot_general(a, b, (((1,), (1,)), ((), ())), preferred_element_type=F32, precision=precision)


def _rope_tile(y, cos, sin, width):
    if width == LANES:
        partner = pltpu.roll(y, LANES // 2, 1)
    else:
        lane = lax.broadcasted_iota(jnp.int32, y.shape, 1)
        half = width // 2
        partner = jnp.where(lane % width < half, pltpu.roll(y, LANES - half, 1), pltpu.roll(y, half, 1))
    return y * cos + partner * sin


def _proj_kernel(*refs, n_w, norm_in, epis, col_chunk):
    it = iter(refs)
    x_ref = next(it)
    g_ref = next(it) if norm_in else None
    w_refs = [next(it) for _ in range(n_w)]
    e_refs = []
    for e in epis:
        if e in ("rope128", "rope64"):
            e_refs.append((next(it), next(it)))
        elif e == "rms":
            e_refs.append((next(it),))
        else:
            e_refs.append(())
    o_refs = [next(it) for _ in range(n_w)]

    x = x_ref[...]
    if norm_in:
        x = _rms(x, g_ref[...])
    xb = x.astype(BF16)
    for w_ref, e, er, o_ref in zip(w_refs, epis, e_refs, o_refs):
        n = w_ref.shape[1]
        if e == "rms":
            o_ref[...] = _rms(_dot(xb, w_ref[...]), er[0][...])
            continue
        for c0 in range(0, n, col_chunk):
            c1 = min(n, c0 + col_chunk)
            y = _dot(xb, w_ref[:, c0:c1])
            if e == "none":
                o_ref[:, c0:c1] = y
            else:
                width = LANES if e == "rope128" else DIFF_DIM
                cos, sin = er[0][...], er[1][...]
                for t0 in range(0, c1 - c0, LANES):
                    o_ref[:, c0 + t0:c0 + t0 + LANES] = _rope_tile(y[:, t0:t0 + LANES], cos, sin, width)


def _proj(x, gain, weights, epis, extras, out_widths, tm):
    m, k = x.shape
    assert m % tm == 0
    row = lambda i: (i, 0)
    fixed = lambda i: (0, 0)
    args, specs = [x], [pl.BlockSpec((tm, k), row)]
    if gain is not None:
        args.append(gain.reshape(1, k))
        specs.append(pl.BlockSpec((1, k), fixed))
    for w in weights:
        args.append(w)
        specs.append(pl.BlockSpec(w.shape, fixed, pipeline_mode=pl.Buffered(1)))
    for e, ex in zip(epis, extras):
        if e in ("rope128", "rope64"):
            for t in ex:
                args.append(t)
                specs.append(pl.BlockSpec((tm, LANES), row))
        elif e == "rms":
            args.append(ex[0].reshape(1, -1))
            specs.append(pl.BlockSpec((1, ex[0].size), fixed))
    out_shape = [jax.ShapeDtypeStruct((m, n), F32) for n in out_widths]
    out_specs = [pl.BlockSpec((tm, n), row) for n in out_widths]
    kern = functools.partial(_proj_kernel, n_w=len(weights), norm_in=gain is not None, epis=tuple(epis),
                             col_chunk=512)
    return pl.pallas_call(kern, grid=(m // tm,), in_specs=specs, out_specs=out_specs, out_shape=out_shape,
                          compiler_params=_cparams(("parallel",)))(*args)


def _outproj_kernel(*refs, n_a):
    a_refs = refs[:n_a]
    w_refs = refs[n_a:2 * n_a]
    g_ref, x_ref, o_ref = refs[2 * n_a:]
    y = _dot(a_refs[0][...].astype(BF16), w_refs[0][...])
    for a_ref, w_ref in zip(a_refs[1:], w_refs[1:]):
        y = y + _dot(a_ref[...].astype(BF16), w_ref[...])
    o_ref[...] = x_ref[...] + _rms(y, g_ref[...])


def _outproj(acts, weights, gain, resid, tm):
    m, d = resid.shape
    row = lambda i: (i, 0)
    fixed = lambda i: (0, 0)
    specs = [pl.BlockSpec((tm, a.shape[1]), row) for a in acts]
    specs += [pl.BlockSpec(w.shape, fixed, pipeline_mode=pl.Buffered(1)) for w in weights]
    specs += [pl.BlockSpec((1, d), fixed), pl.BlockSpec((tm, d), row)]
    return pl.pallas_call(functools.partial(_outproj_kernel, n_a=len(acts)), grid=(m // tm,), in_specs=specs,
                          out_specs=pl.BlockSpec((tm, d), row), out_shape=jax.ShapeDtypeStruct((m, d), F32),
                          compiler_params=_cparams(("parallel",)))(*acts, *weights, gain.reshape(1, d), resid)


def _ffn_kernel(x_ref, g_in_ref, wg_ref, wu_ref, cw_ref, cb_ref, wd_ref, g_out_ref, past_ref,
                y_ref, tail_ref, h_s, acc_s, gbuf_s, carry_s, *, blocks_per_seq):
    i, f = pl.program_id(0), pl.program_id(1)
    n_seq, t_blk, tf = gbuf_s.shape[0], gbuf_s.shape[1] - SUBLANES, gbuf_s.shape[2]
    halo = CONV_W - 1

    @pl.when(f == 0)
    def _():
        h_s[...] = _rms(x_ref[...], g_in_ref[...]).astype(BF16)
        acc_s[...] = jnp.zeros_like(acc_s)

    h = h_s[...]
    g = _dot(h, wg_ref[...])
    u = _dot(h, wu_ref[...])

    if blocks_per_seq == 1:
        gbuf_s[:, 0:SUBLANES, :] = past_ref[...]
    else:
        @pl.when(i % blocks_per_seq == 0)
        def _():
            gbuf_s[:, 0:SUBLANES, :] = past_ref[...]

        @pl.when(i % blocks_per_seq != 0)
        def _():
            gbuf_s[:, 0:SUBLANES, :] = carry_s[f]
    gbuf_s[:, SUBLANES:, :] = g.reshape(n_seq, t_blk, tf)
    if blocks_per_seq > 1:
        carry_s[f] = gbuf_s[:, t_blk:, :]
    tail_ref[...] = gbuf_s[:, t_blk + SUBLANES - halo:, :]

    gc = cb_ref[...].reshape(1, 1, tf)
    for j in range(CONV_W):
        gc = gc + cw_ref[j:j + 1, :].reshape(1, 1, tf) * gbuf_s[:, SUBLANES - halo + j:SUBLANES - halo + j + t_blk, :]
    act = jax.nn.gelu(gc.reshape(n_seq * t_blk, tf), approximate=True) * u
    acc_s[...] += _dot(act.astype(BF16), wd_ref[...])

    @pl.when(f == pl.num_programs(1) - 1)
    def _():
        y_ref[...] = x_ref[...] + _rms(acc_s[...], g_out_ref[...])


def _ffn(x, past8, g_in, wg, wu, cw, cb, wd, g_out, seq_len, tm, tf):
    m, d = x.shape
    n_seq = m // seq_len
    if seq_len >= tm:
        blocks_per_seq, seq_per_blk, t_blk = seq_len // tm, 1, tm
    else:
        blocks_per_seq, seq_per_blk, t_blk = 1, tm // seq_len, seq_len
    n_f = D_FF // tf
    halo = CONV_W - 1
    seq_idx = lambda i, f: (i // blocks_per_seq, 0, f)
    in_specs = [
        pl.BlockSpec((tm, d), lambda i, f: (i, 0)),
        pl.BlockSpec((1, d), lambda i, f: (0, 0)),
        pl.BlockSpec((d, tf), lambda i, f: (0, f)),
        pl.BlockSpec((d, tf), lambda i, f: (0, f)),
        pl.BlockSpec((CONV_W, tf), lambda i, f: (0, f)),
        pl.BlockSpec((1, tf), lambda i, f: (0, f)),
        pl.BlockSpec((tf, d), lambda i, f: (f, 0)),
        pl.BlockSpec((1, d), lambda i, f: (0, 0)),
        pl.BlockSpec((seq_per_blk, SUBLANES, tf), seq_idx),
    ]
    out_specs = [pl.BlockSpec((tm, d), lambda i, f: (i, 0)),
                 pl.BlockSpec((seq_per_blk, halo, tf), lambda i, f: (i, 0, f))]
    out_shape = [jax.ShapeDtypeStruct((m, d), F32),
                 jax.ShapeDtypeStruct((m // tm * seq_per_blk, halo, D_FF), F32)]
    scratch = [pltpu.VMEM((tm, d), BF16), pltpu.VMEM((tm, d), F32),
               pltpu.VMEM((seq_per_blk, t_blk + SUBLANES, tf), F32),
               pltpu.VMEM((n_f, seq_per_blk, SUBLANES, tf), F32)]
    kern = functools.partial(_ffn_kernel, blocks_per_seq=blocks_per_seq)
    y, tails = pl.pallas_call(kern, grid=(m // tm, n_f), in_specs=in_specs, out_specs=out_specs,
                              out_shape=out_shape, scratch_shapes=scratch,
                              compiler_params=_cparams(("arbitrary", "arbitrary")))(
        x, g_in.reshape(1, d), wg, wu, cw, cb.reshape(1, D_FF), wd, g_out.reshape(1, d), past8)
    return y, tails[blocks_per_seq - 1::blocks_per_seq]


def _softmax_init(s, v, m_s, l_s, acc_s):
    m = jnp.max(s, axis=-1, keepdims=True)
    p = jnp.exp(s - m)
    m_s[...] = m
    l_s[...] = jnp.sum(p, axis=-1, keepdims=True)
    acc_s[...] = _dot(p.astype(BF16), v)


def _softmax_update(s, v, m_s, l_s, acc_s):
    m_old = m_s[...]
    m_new = jnp.maximum(m_old, jnp.max(s, axis=-1, keepdims=True))
    alpha = jnp.exp(m_old - m_new)
    p = jnp.exp(s - m_new)
    m_s[...] = m_new
    l_s[...] = alpha * l_s[...] + jnp.sum(p, axis=-1, keepdims=True)
    acc_s[...] = alpha * acc_s[...] + _dot(p.astype(BF16), v)


def _topk_mask(gate, valid, k):
    lane = lax.broadcasted_iota(jnp.int32, gate.shape, 1)
    big = jnp.int32(gate.shape[1])
    sel = jnp.zeros(gate.shape, jnp.bool_)
    rem = valid
    for _ in range(k):
        gm = jnp.where(rem, gate, NEG_INF)
        top = jnp.max(gm, axis=-1, keepdims=True)
        idx = jnp.min(jnp.where(rem & (gm == top), lane, big), axis=-1, keepdims=True)
        pick = lane == idx
        sel = sel | pick
        rem = rem & jnp.logical_not(pick)
    return sel


def _causal_tile(rows, cols, period):
    r = lax.broadcasted_iota(jnp.int32, (rows, cols), 0) % period
    c = lax.broadcasted_iota(jnp.int32, (rows, cols), 1)
    return c <= r


def _diff_lambda(lam_ref, lam_init):
    lv = lam_ref[...]
    d1 = jnp.sum(lv[0:1] * lv[1:2], axis=-1, keepdims=True)
    d2 = jnp.sum(lv[2:3] * lv[3:4], axis=-1, keepdims=True)
    return jnp.exp(d1) - jnp.exp(d2) + lam_init


def _split_maps(q2):
    lane = lax.broadcasted_iota(jnp.int32, q2.shape, 1)
    lo = jnp.where(lane < DIFF_DIM, q2, 0.0)
    hi = jnp.where(lane >= DIFF_DIM, q2, 0.0)
    return jnp.concatenate([lo, hi], axis=0)


TQ = MOBA_BLOCK


def _moba_prefill_kernel(q_ref, k_ref, v_ref, o_ref, kb_s, vb_s, selb_s, m_s, l_s, acc_s):
    qi = pl.program_id(2)
    t = k_ref.shape[0]
    nb = t // MOBA_BLOCK
    n_sel = max(1, min(MOBA_TOPK, nb - 1))
    scale = HEAD_DIM ** -0.5
    g_per = MOBA_HEADS // MOBA_KV_HEADS

    @pl.when(qi == 0)
    def _():
        kb_s[...] = k_ref[...].astype(BF16)
        vb_s[...] = v_ref[...].astype(BF16)

    kmean = jnp.mean(k_ref[...].reshape(nb, MOBA_BLOCK, HEAD_DIM), axis=1)
    tril = _causal_tile(TQ, TQ, TQ)
    blk = lax.broadcasted_iota(jnp.int32, (TQ, nb), 1)
    own0 = pl.multiple_of(qi * TQ, TQ)
    for g in range(g_per):
        qf = q_ref[:, g * HEAD_DIM:(g + 1) * HEAD_DIM]
        qb = qf.astype(BF16)
        gate = _dot_nt(qf, kmean, lax.Precision.HIGHEST)
        sel = _topk_mask(gate, blk < qi, n_sel)
        for j in range(nb):
            selb_s[j] = jnp.broadcast_to(sel[:, j:j + 1], (TQ, LANES)).astype(F32)
        s_own = _dot_nt(qb, kb_s[pl.ds(own0, TQ), :]) * scale
        _softmax_init(jnp.where(tril, s_own, NEG_INF), vb_s[pl.ds(own0, TQ), :], m_s, l_s, acc_s)

        def body(j, carry):
            k0 = pl.multiple_of(j * TQ, TQ)
            s = _dot_nt(qb, kb_s[pl.ds(k0, TQ), :]) * scale
            keep = jnp.concatenate([selb_s[j]] * (TQ // LANES), axis=1) > 0.5
            _softmax_update(jnp.where(keep, s, NEG_INF), vb_s[pl.ds(k0, TQ), :], m_s, l_s, acc_s)
            return carry

        lax.fori_loop(0, qi, body, 0)
        o_ref[:, g * HEAD_DIM:(g + 1) * HEAD_DIM] = acc_s[...] / l_s[...]


def _moba_prefill(mq, mk, mv, n_batch, t):
    m = mq.shape[0]
    nq = t // TQ
    g_w = MOBA_HEADS // MOBA_KV_HEADS * HEAD_DIM
    q_spec = pl.BlockSpec((TQ, g_w), lambda b, k, i: (b * nq + i, k))
    kv_spec = pl.BlockSpec((t, HEAD_DIM), lambda b, k, i: (b, k))
    scratch = [pltpu.VMEM((t, HEAD_DIM), BF16), pltpu.VMEM((t, HEAD_DIM), BF16),
               pltpu.VMEM((t // MOBA_BLOCK, TQ, LANES), F32),
               pltpu.VMEM((TQ, 1), F32), pltpu.VMEM((TQ, 1), F32), pltpu.VMEM((TQ, HEAD_DIM), F32)]
    return pl.pallas_call(_moba_prefill_kernel, grid=(n_batch, MOBA_KV_HEADS, nq),
                          in_specs=[q_spec, kv_spec, kv_spec], out_specs=q_spec,
                          out_shape=jax.ShapeDtypeStruct((m, MOBA_HEADS * HEAD_DIM), F32), scratch_shapes=scratch,
                          compiler_params=_cparams(("parallel", "parallel", "arbitrary")))(mq, mk, mv)


def _diff_finish(o2, lam, sub_g, lam_init, rows):
    o = o2[:rows] - lam * o2[rows:]
    return _rms(o, sub_g) * (1.0 - lam_init)


def _diff_prefill_kernel(q_ref, k_ref, v_ref, lam_ref, subg_ref, o_ref, kb_s, vb_s, m_s, l_s, acc_s, *, lam_init):
    qi = pl.program_id(2)
    scale = DIFF_DIM ** -0.5
    g_per = DIFF_HEADS // DIFF_KV_HEADS
    width = 2 * DIFF_DIM

    @pl.when(qi == 0)
    def _():
        kb_s[...] = k_ref[...].astype(BF16)
        vb_s[...] = v_ref[...].astype(BF16)

    lam = _diff_lambda(lam_ref, lam_init)
    tril = _causal_tile(2 * TQ, TQ, TQ)
    own0 = pl.multiple_of(qi * TQ, TQ)
    for g in range(g_per):
        qb = _split_maps(q_ref[:, g * width:(g + 1) * width]).astype(BF16)
        s_own = _dot_nt(qb, kb_s[pl.ds(own0, TQ), :]) * scale
        _softmax_init(jnp.where(tril, s_own, NEG_INF), vb_s[pl.ds(own0, TQ), :], m_s, l_s, acc_s)

        def body(j, carry):
            k0 = pl.multiple_of(j * TQ, TQ)
            s = _dot_nt(qb, kb_s[pl.ds(k0, TQ), :]) * scale
            _softmax_update(s, vb_s[pl.ds(k0, TQ), :], m_s, l_s, acc_s)
            return carry

        lax.fori_loop(0, qi, body, 0)
        o_ref[:, g * width:(g + 1) * width] = _diff_finish(acc_s[...] / l_s[...], lam, subg_ref[...], lam_init, TQ)


def _diff_prefill(dq, dk, dv, lam_vec, sub_g, lam_init, n_batch, t):
    m = dq.shape[0]
    nq = t // TQ
    width = 2 * DIFF_DIM
    g_w = DIFF_HEADS // DIFF_KV_HEADS * width
    q_spec = pl.BlockSpec((TQ, g_w), lambda b, k, i: (b * nq + i, k))
    kv_spec = pl.BlockSpec((t, width), lambda b, k, i: (b, k))
    fixed = lambda b, k, i: (0, 0)
    scratch = [pltpu.VMEM((t, width), BF16), pltpu.VMEM((t, width), BF16),
               pltpu.VMEM((2 * TQ, 1), F32), pltpu.VMEM((2 * TQ, 1), F32), pltpu.VMEM((2 * TQ, width), F32)]
    kern = functools.partial(_diff_prefill_kernel, lam_init=lam_init)
    return pl.pallas_call(kern, grid=(n_batch, DIFF_KV_HEADS, nq),
                          in_specs=[q_spec, kv_spec, kv_spec, pl.BlockSpec((4, DIFF_DIM), fixed),
                                    pl.BlockSpec((1, width), fixed)],
                          out_specs=q_spec, out_shape=jax.ShapeDtypeStruct((m, DIFF_HEADS * width), F32),
                          scratch_shapes=scratch,
                          compiler_params=_cparams(("parallel", "parallel", "arbitrary")))(
        dq, dk, dv, lam_vec, sub_g.reshape(1, width))


def _mla_prefill_kernel(qn_ref, qr_ref, ckv_ref, kr2_ref, wuk_ref, wuv_ref, o_ref, cb_s, kb_s, m_s, l_s, acc_s):
    qi = pl.program_id(1)
    scale = (NOPE_DIM + ROPE_DIM) ** -0.5

    @pl.when(qi == 0)
    def _():
        cb_s[...] = ckv_ref[...].astype(BF16)
        kb_s[...] = kr2_ref[...].astype(BF16)

    tril = _causal_tile(TQ, TQ, TQ)
    own0 = pl.multiple_of(qi * TQ, TQ)
    lane = lax.broadcasted_iota(jnp.int32, (TQ, LANES), 1)
    for h in range(MLA_HEADS):
        q_lat = _dot(qn_ref[:, h * NOPE_DIM:(h + 1) * NOPE_DIM].astype(BF16), wuk_ref[h]).astype(BF16)
        pair = qr_ref[:, (h // 2) * LANES:(h // 2 + 1) * LANES]
        in_head = (lane < ROPE_DIM) if h % 2 == 0 else (lane >= ROPE_DIM)
        q_r = jnp.where(in_head, pair, 0.0).astype(BF16)

        def scores(k0):
            return (_dot_nt(q_lat, cb_s[pl.ds(k0, TQ), :]) + _dot_nt(q_r, kb_s[pl.ds(k0, TQ), :])) * scale

        _softmax_init(jnp.where(tril, scores(own0), NEG_INF), cb_s[pl.ds(own0, TQ), :], m_s, l_s, acc_s)

        def body(j, carry):
            k0 = pl.multiple_of(j * TQ, TQ)
            _softmax_update(scores(k0), cb_s[pl.ds(k0, TQ), :], m_s, l_s, acc_s)
            return carry

        lax.fori_loop(0, qi, body, 0)
        o_lat = (acc_s[...] / l_s[...]).astype(BF16)
        o_ref[:, h * V_DIM:(h + 1) * V_DIM] = _dot(o_lat, wuv_ref[h])


def _mla_prefill(q_nope, q_rope, ckv, kr2, wuk_t, wuv_t, n_batch, t):
    m = q_nope.shape[0]
    nq = t // TQ
    row = lambda b, i: (b * nq + i, 0)
    seq = lambda b, i: (b, 0)
    fixed3 = lambda b, i: (0, 0, 0)
    scratch = [pltpu.VMEM((t, KV_LORA), BF16), pltpu.VMEM((t, LANES), BF16),
               pltpu.VMEM((TQ, 1), F32), pltpu.VMEM((TQ, 1), F32), pltpu.VMEM((TQ, KV_LORA), F32)]
    in_specs = [pl.BlockSpec((TQ, MLA_HEADS * NOPE_DIM), row), pl.BlockSpec((TQ, MLA_HEADS * ROPE_DIM), row),
                pl.BlockSpec((t, KV_LORA), seq), pl.BlockSpec((t, LANES), seq),
                pl.BlockSpec(wuk_t.shape, fixed3, pipeline_mode=pl.Buffered(1)),
                pl.BlockSpec(wuv_t.shape, fixed3, pipeline_mode=pl.Buffered(1))]
    return pl.pallas_call(_mla_prefill_kernel, grid=(n_batch, nq), in_specs=in_specs,
                          out_specs=pl.BlockSpec((TQ, MLA_HEADS * V_DIM), row),
                          out_shape=jax.ShapeDtypeStruct((m, MLA_HEADS * V_DIM), F32), scratch_shapes=scratch,
                          compiler_params=_cparams(("parallel", "arbitrary")))(q_nope, q_rope, ckv, kr2, wuk_t, wuv_t)


PAGES_PER_STEP = 8
N_PAGES = PAST_LEN // PAGE_SIZE


def _page_specs(width, pages_per_step):
    def spec(u):
        return pl.BlockSpec((None, PAGE_SIZE, width),
                            lambda s, c, pt: (pt[s * N_PAGES + c * pages_per_step + u], 0, 0))
    return [spec(u) for u in range(pages_per_step)]


def _gather_pages(refs, c0, c1):
    return jnp.concatenate([r[:, c0:c1] for r in refs], axis=0).astype(BF16)


def _rows_by_head(ref, heads, width):
    return jnp.concatenate([ref[:, h * width:(h + 1) * width] for h in heads], axis=0)


def _moba_decode_kernel(pt_ref, q_ref, kn_ref, vn_ref, *refs, pages_per_step, t_new):
    p = pages_per_step
    k_refs, v_refs = refs[:p], refs[p:2 * p]
    o_ref, m_s, l_s, o_s, km_s = refs[2 * p:]
    c = pl.program_id(1)
    scale = HEAD_DIM ** -0.5
    g_per = MOBA_HEADS // MOBA_KV_HEADS
    rows = g_per * t_new
    pages_per_blk = MOBA_BLOCK // PAGE_SIZE
    blk_per_step = p // pages_per_blk
    n_past_blk = PAST_LEN // MOBA_BLOCK
    n_sel = max(1, min(MOBA_TOPK, n_past_blk))

    for kvh in range(MOBA_KV_HEADS):
        c0, c1 = kvh * HEAD_DIM, (kvh + 1) * HEAD_DIM
        q_rows = _rows_by_head(q_ref, range(kvh * g_per, (kvh + 1) * g_per), HEAD_DIM)
        qb = q_rows.astype(BF16)
        for b in range(blk_per_step):
            kf = jnp.concatenate([r[:, c0:c1] for r in k_refs[b * pages_per_blk:(b + 1) * pages_per_blk]], axis=0)
            vb = _gather_pages(v_refs[b * pages_per_blk:(b + 1) * pages_per_blk], c0, c1)
            s = _dot_nt(qb, kf.astype(BF16)) * scale
            m = jnp.max(s, axis=-1, keepdims=True)
            e = jnp.exp(s - m)
            blk = c * blk_per_step + b
            m_s[kvh, blk] = jnp.broadcast_to(m, (rows, LANES))
            l_s[kvh, blk] = jnp.broadcast_to(jnp.sum(e, axis=-1, keepdims=True), (rows, LANES))
            o_s[kvh, blk] = _dot(e.astype(BF16), vb)
            km_s[kvh, pl.ds(blk, 1), :] = jnp.mean(kf, axis=0, keepdims=True)

    @pl.when(c == pl.num_programs(1) - 1)
    def _():
        causal = _causal_tile(rows, t_new, t_new)
        for kvh in range(MOBA_KV_HEADS):
            c0, c1 = kvh * HEAD_DIM, (kvh + 1) * HEAD_DIM
            heads = range(kvh * g_per, (kvh + 1) * g_per)
            q_rows = _rows_by_head(q_ref, heads, HEAD_DIM)
            qb = q_rows.astype(BF16)
            gate = _dot_nt(q_rows, km_s[kvh], lax.Precision.HIGHEST)
            sel = _topk_mask(gate, jnp.ones(gate.shape, jnp.bool_), n_sel)
            s_own = jnp.where(causal, _dot_nt(qb, kn_ref[:, c0:c1].astype(BF16)) * scale, NEG_INF)
            m_fin = jnp.broadcast_to(jnp.max(s_own, axis=-1, keepdims=True), (rows, LANES))
            for b in range(n_past_blk):
                m_fin = jnp.maximum(m_fin, jnp.where(sel[:, b:b + 1], m_s[kvh, b], NEG_INF))
            e_own = jnp.exp(s_own - m_fin[:, :1])
            l_fin = jnp.broadcast_to(jnp.sum(e_own, axis=-1, keepdims=True), (rows, LANES))
            acc = _dot(e_own.astype(BF16), vn_ref[:, c0:c1].astype(BF16))
            for b in range(n_past_blk):
                w = jnp.exp(jnp.where(sel[:, b:b + 1], m_s[kvh, b] - m_fin, NEG_INF))
                l_fin = l_fin + w * l_s[kvh, b]
                acc = acc + w * o_s[kvh, b]
            o = acc / l_fin
            for i, h in enumerate(heads):
                o_ref[:, h * HEAD_DIM:(h + 1) * HEAD_DIM] = o[i * t_new:(i + 1) * t_new]


def _moba_decode(mq, mk, mv, pool_k, pool_v, page_table, t_new):
    n_seq = mq.shape[0] // t_new
    p = PAGES_PER_STEP
    kv_w = MOBA_KV_HEADS * HEAD_DIM
    g_per = MOBA_HEADS // MOBA_KV_HEADS
    rows = g_per * t_new
    n_blk = PAST_LEN // MOBA_BLOCK
    seq = lambda s, c, pt: (s, 0)
    in_specs = [pl.BlockSpec((t_new, MOBA_HEADS * HEAD_DIM), seq), pl.BlockSpec((t_new, kv_w), seq),
                pl.BlockSpec((t_new, kv_w), seq)] + _page_specs(kv_w, p) + _page_specs(kv_w, p)
    scratch = [pltpu.VMEM((MOBA_KV_HEADS, n_blk, rows, LANES), F32), pltpu.VMEM((MOBA_KV_HEADS, n_blk, rows, LANES), F32),
               pltpu.VMEM((MOBA_KV_HEADS, n_blk, rows, HEAD_DIM), F32), pltpu.VMEM((MOBA_KV_HEADS, n_blk, HEAD_DIM), F32)]
    grid_spec = pltpu.PrefetchScalarGridSpec(
        num_scalar_prefetch=1, grid=(n_seq, N_PAGES // p), in_specs=in_specs,
        out_specs=pl.BlockSpec((t_new, MOBA_HEADS * HEAD_DIM), seq), scratch_shapes=scratch)
    kern = functools.partial(_moba_decode_kernel, pages_per_step=p, t_new=t_new)
    return pl.pallas_call(kern, grid_spec=grid_spec, out_shape=jax.ShapeDtypeStruct(mq.shape, F32),
                          compiler_params=_cparams(("parallel", "arbitrary")))(
        page_table.reshape(-1), mq, mk, mv, *([pool_k] * p), *([pool_v] * p))


def _diff_decode_kernel(pt_ref, q_ref, kn_ref, vn_ref, lam_ref, subg_ref, *refs, pages_per_step, t_new, lam_init):
    p = pages_per_step
    k_refs, v_refs = refs[:p], refs[p:2 * p]
    o_ref, m_s, l_s, acc_s = refs[2 * p:]
    c = pl.program_id(1)
    scale = DIFF_DIM ** -0.5
    g_per = DIFF_HEADS // DIFF_KV_HEADS
    width = 2 * DIFF_DIM
    rows = g_per * t_new
    causal = _causal_tile(2 * rows, t_new, t_new)

    for kvh in range(DIFF_KV_HEADS):
        c0, c1 = kvh * width, (kvh + 1) * width
        heads = range(kvh * g_per, (kvh + 1) * g_per)
        qb = _split_maps(_rows_by_head(q_ref, heads, width)).astype(BF16)
        m_k, l_k, acc_k = m_s.at[kvh], l_s.at[kvh], acc_s.at[kvh]

        @pl.when(c == 0)
        def _():
            s_own = jnp.where(causal, _dot_nt(qb, kn_ref[:, c0:c1].astype(BF16)) * scale, NEG_INF)
            _softmax_init(s_own, vn_ref[:, c0:c1].astype(BF16), m_k, l_k, acc_k)

        s = _dot_nt(qb, _gather_pages(k_refs, c0, c1)) * scale
        _softmax_update(s, _gather_pages(v_refs, c0, c1), m_k, l_k, acc_k)

        @pl.when(c == pl.num_programs(1) - 1)
        def _():
            lam = _diff_lambda(lam_ref, lam_init)
            o = _diff_finish(acc_k[...] / l_k[...], lam, subg_ref[...], lam_init, rows)
            for i, h in enumerate(heads):
                o_ref[:, h * width:(h + 1) * width] = o[i * t_new:(i + 1) * t_new]


def _diff_decode(dq, dk, dv, pool_k, pool_v, page_table, lam_vec, sub_g, lam_init, t_new):
    n_seq = dq.shape[0] // t_new
    p = PAGES_PER_STEP
    width = 2 * DIFF_DIM
    kv_w = DIFF_KV_HEADS * width
    rows2 = 2 * (DIFF_HEADS // DIFF_KV_HEADS) * t_new
    seq = lambda s, c, pt: (s, 0)
    fixed = lambda s, c, pt: (0, 0)
    in_specs = [pl.BlockSpec((t_new, DIFF_HEADS * width), seq), pl.BlockSpec((t_new, kv_w), seq),
                pl.BlockSpec((t_new, kv_w), seq), pl.BlockSpec((4, DIFF_DIM), fixed),
                pl.BlockSpec((1, width), fixed)] + _page_specs(kv_w, p) + _page_specs(kv_w, p)
    scratch = [pltpu.VMEM((DIFF_KV_HEADS, rows2, 1), F32), pltpu.VMEM((DIFF_KV_HEADS, rows2, 1), F32),
               pltpu.VMEM((DIFF_KV_HEADS, rows2, width), F32)]
    grid_spec = pltpu.PrefetchScalarGridSpec(
        num_scalar_prefetch=1, grid=(n_seq, N_PAGES // p), in_specs=in_specs,
        out_specs=pl.BlockSpec((t_new, DIFF_HEADS * width), seq), scratch_shapes=scratch)
    kern = functools.partial(_diff_decode_kernel, pages_per_step=p, t_new=t_new, lam_init=lam_init)
    return pl.pallas_call(kern, grid_spec=grid_spec, out_shape=jax.ShapeDtypeStruct(dq.shape, F32),
                          compiler_params=_cparams(("parallel", "arbitrary")))(
        page_table.reshape(-1), dq, dk, dv, lam_vec, sub_g.reshape(1, width), *([pool_k] * p), *([pool_v] * p))


def _mla_decode_kernel(pt_ref, qn_ref, qr_ref, cn_ref, krn_ref, wuk_ref, wuv_ref, *refs, pages_per_step, t_new):
    p = pages_per_step
    c_refs, r_refs = refs[:p], refs[p:2 * p]
    o_ref, qlat_s, m_s, l_s, acc_s = refs[2 * p:]
    c = pl.program_id(1)
    scale = (NOPE_DIM + ROPE_DIM) ** -0.5
    rows = MLA_HEADS * t_new

    @pl.when(c == 0)
    def _():
        for h in range(MLA_HEADS):
            qlat_s[h * t_new:(h + 1) * t_new, :] = _dot(
                qn_ref[:, h * NOPE_DIM:(h + 1) * NOPE_DIM].astype(BF16), wuk_ref[h])
        q_lat = qlat_s[...].astype(BF16)
        causal = _causal_tile(rows, t_new, t_new)
        cn = cn_ref[...].astype(BF16)
        s_own = (_dot_nt(q_lat, cn) + _dot_nt(qr_ref[...].astype(BF16), krn_ref[...].astype(BF16))) * scale
        _softmax_init(jnp.where(causal, s_own, NEG_INF), cn, m_s, l_s, acc_s)

    q_lat = qlat_s[...].astype(BF16)
    ckv = _gather_pages(c_refs, 0, KV_LORA)
    s = (_dot_nt(q_lat, ckv) + _dot_nt(qr_ref[...].astype(BF16), _gather_pages(r_refs, 0, ROPE_DIM))) * scale
    _softmax_update(s, ckv, m_s, l_s, acc_s)

    @pl.when(c == pl.num_programs(1) - 1)
    def _():
        o_lat = (acc_s[...] / l_s[...]).astype(BF16)
        for h in range(MLA_HEADS):
            o_ref[:, h * V_DIM:(h + 1) * V_DIM] = _dot(o_lat[h * t_new:(h + 1) * t_new], wuv_ref[h])


def _mla_decode(q_nope, q_rope_rows, ckv_new, kr_new, pool_c, pool_r, page_table, wuk_t, wuv_t, t_new):
    n_seq = q_nope.shape[0] // t_new
    p = PAGES_PER_STEP
    rows = MLA_HEADS * t_new
    seq = lambda s, c, pt: (s, 0)
    fixed3 = lambda s, c, pt: (0, 0, 0)
    in_specs = [pl.BlockSpec((t_new, MLA_HEADS * NOPE_DIM), seq),
                pl.BlockSpec((None, rows, ROPE_DIM), lambda s, c, pt: (s, 0, 0)),
                pl.BlockSpec((t_new, KV_LORA), seq), pl.BlockSpec((t_new, ROPE_DIM), seq),
                pl.BlockSpec(wuk_t.shape, fixed3, pipeline_mode=pl.Buffered(1)),
                pl.BlockSpec(wuv_t.shape, fixed3, pipeline_mode=pl.Buffered(1))]
    in_specs += _page_specs(KV_LORA, p) + _page_specs(ROPE_DIM, p)
    scratch = [pltpu.VMEM((rows, KV_LORA), F32), pltpu.VMEM((rows, 1), F32), pltpu.VMEM((rows, 1), F32),
               pltpu.VMEM((rows, KV_LORA), F32)]
    grid_spec = pltpu.PrefetchScalarGridSpec(
        num_scalar_prefetch=1, grid=(n_seq, N_PAGES // p), in_specs=in_specs,
        out_specs=pl.BlockSpec((t_new, MLA_HEADS * V_DIM), seq), scratch_shapes=scratch)
    kern = functools.partial(_mla_decode_kernel, pages_per_step=p, t_new=t_new)
    return pl.pallas_call(kern, grid_spec=grid_spec,
                          out_shape=jax.ShapeDtypeStruct((n_seq * t_new, MLA_HEADS * V_DIM), F32),
                          compiler_params=_cparams(("parallel", "arbitrary")))(
        page_table.reshape(-1), q_nope, q_rope_rows, ckv_new, kr_new, wuk_t, wuv_t, *([pool_c] * p), *([pool_r] * p))


ROW_TILE = 512
FF_TILE = 512


def _rope_tables(pos, width):
    inv = 1.0 / (ROPE_THETA ** (jnp.arange(0, width, 2, dtype=F32) / width))
    ang = pos.astype(F32)[:, None] * inv
    cos, sin = jnp.cos(ang), jnp.sin(ang)
    reps = LANES // width
    return jnp.tile(jnp.concatenate([cos, cos], axis=1), (1, reps)), jnp.tile(jnp.concatenate([-sin, sin], axis=1), (1, reps))


def _prep_weights(w_in_ab, w_out_ab, mla_w_dq, mla_w_uq, mla_w_dkv, mla_w_uk, mla_w_uv, mla_w_o,
                  ffn_w_gate, ffn_w_up, ffn_w_down):
    bf = lambda w: w.astype(BF16)
    cols = (MOBA_HEADS * HEAD_DIM, MOBA_KV_HEADS * HEAD_DIM, MOBA_KV_HEADS * HEAD_DIM,
            DIFF_HEADS * 2 * DIFF_DIM, DIFF_KV_HEADS * 2 * DIFF_DIM, DIFF_KV_HEADS * 2 * DIFF_DIM)
    offs = [sum(cols[:i]) for i in range(len(cols) + 1)]
    w_in = bf(w_in_ab[0])
    w_out = bf(w_out_ab[0])
    uq = bf(mla_w_uq[0]).reshape(-1, MLA_HEADS, NOPE_DIM + ROPE_DIM)
    dkv = bf(mla_w_dkv[0])
    return dict(
        ab_in=[w_in[:, offs[i]:offs[i + 1]] for i in range(len(cols))],
        ab_out=[w_out[:MOBA_HEADS * HEAD_DIM], w_out[MOBA_HEADS * HEAD_DIM:]],
        dq=bf(mla_w_dq[0]),
        uq_nope=uq[:, :, :NOPE_DIM].reshape(-1, MLA_HEADS * NOPE_DIM),
        uq_rope=uq[:, :, NOPE_DIM:].reshape(-1, MLA_HEADS * ROPE_DIM),
        dkv_c=dkv[:, :KV_LORA],
        dkv_r2=jnp.concatenate([dkv[:, KV_LORA:], dkv[:, KV_LORA:]], axis=1),
        uk_t=bf(mla_w_uk[0]).transpose(1, 2, 0),
        uv_t=bf(mla_w_uv[0]).transpose(1, 0, 2),
        o=bf(mla_w_o[0]),
        gate=bf(ffn_w_gate), up=bf(ffn_w_up), down=bf(ffn_w_down),
    )


def _trunk(x3, pos0, w, prm, past):
    n_seq, t, d = x3.shape
    x = x3.reshape(n_seq * t, d)
    pos = jnp.tile(pos0 + jnp.arange(t, dtype=jnp.int32), n_seq)
    rope128 = _rope_tables(pos, HEAD_DIM)
    rope64 = _rope_tables(pos, DIFF_DIM)
    gains = prm["norm_gains"]
    lam_init0 = 0.8 - 0.6 * math.exp(-0.3 * 0)
    halo = CONV_W - 1

    def ffn(xin, layer):
        if past is None:
            past8 = jnp.zeros((n_seq, SUBLANES, D_FF), F32)
        else:
            past8 = jnp.pad(past["conv"][layer], ((0, 0), (SUBLANES - halo, 0), (0, 0)))
        return _ffn(xin, past8, gains[layer, 2], w["gate"][layer], w["up"][layer], prm["ffn_conv_w"][layer],
                    prm["ffn_conv_b"][layer], w["down"][layer], gains[layer, 3], t, ROW_TILE, FF_TILE)

    mq, mk, mv, dq, dk, dv = _proj(
        x, gains[0, 0], w["ab_in"], ["rope128", "rope128", "none", "rope64", "rope64", "none"],
        [rope128, rope128, (), rope64, rope64, ()], [wi.shape[1] for wi in w["ab_in"]], ROW_TILE // 2)
    if past is None:
        o_m = _moba_prefill(mq, mk, mv, n_seq, t)
        o_d = _diff_prefill(dq, dk, dv, prm["diff_lambda"][0], prm["diff_subln"][0], lam_init0, n_seq, t)
    else:
        pt = past["page_table"]
        flat = lambda pool: pool[0].reshape(pool.shape[1], PAGE_SIZE, -1)
        o_m = _moba_decode(mq, mk, mv, flat(past["moba"][0]), flat(past["moba"][1]), pt, t)
        o_d = _diff_decode(dq, dk, dv, flat(past["diff"][0]), flat(past["diff"][1]), pt,
                           prm["diff_lambda"][0], prm["diff_subln"][0], lam_init0, t)
    x = _outproj([o_m, o_d], w["ab_out"], gains[0, 1], x, ROW_TILE)
    x, conv0 = ffn(x, 0)

    cq, ckv, kr2 = _proj(x, gains[1, 0], [w["dq"], w["dkv_c"], w["dkv_r2"]], ["rms", "rms", "rope64"],
                         [(prm["mla_g_q"][0],), (prm["mla_g_kv"][0],), rope64], [Q_LORA, KV_LORA, LANES], ROW_TILE)
    q_nope, q_rope = _proj(cq, None, [w["uq_nope"], w["uq_rope"]], ["none", "rope64"], [(), rope64],
                           [MLA_HEADS * NOPE_DIM, MLA_HEADS * ROPE_DIM], ROW_TILE)
    kr = kr2[:, :ROPE_DIM]
    if past is None:
        o = _mla_prefill(q_nope, q_rope, ckv, kr2, w["uk_t"], w["uv_t"], n_seq, t)
    else:
        q_rope_rows = q_rope.reshape(n_seq, t, MLA_HEADS, ROPE_DIM).transpose(0, 2, 1, 3).reshape(
            n_seq, MLA_HEADS * t, ROPE_DIM)
        o = _mla_decode(q_nope, q_rope_rows, ckv, kr, past["mla"][0][0], past["mla"][1][0], past["page_table"],
                        w["uk_t"], w["uv_t"], t)
    x = _outproj([o], [w["o"]], gains[1, 1], x, ROW_TILE)
    x, conv1 = ffn(x, 1)

    new = (mk.reshape(1, n_seq, t, MOBA_KV_HEADS, HEAD_DIM), mv.reshape(1, n_seq, t, MOBA_KV_HEADS, HEAD_DIM),
           dk.reshape(1, n_seq, t, DIFF_KV_HEADS, 2, DIFF_DIM), dv.reshape(1, n_seq, t, DIFF_KV_HEADS, 2 * DIFF_DIM),
           ckv.reshape(1, n_seq, t, KV_LORA), kr.reshape(1, n_seq, t, ROPE_DIM), jnp.stack([conv0, conv1]))
    return x.reshape(n_seq, t, d), new


def kernel(x_prompt, x_sample, cache_moba_k, cache_moba_v, cache_diff_k, cache_diff_v, cache_mla_ckv,
           cache_mla_krope, state_ffn_conv, page_table, norm_gains, w_in_ab, w_out_ab, diff_lambda, diff_subln,
           mla_w_dq, mla_g_q, mla_w_uq, mla_w_dkv, mla_g_kv, mla_w_uk, mla_w_uv, mla_w_o,
           ffn_w_gate, ffn_w_up, ffn_conv_w, ffn_conv_b, ffn_w_down):
    w = _prep_weights(w_in_ab, w_out_ab, mla_w_dq, mla_w_uq, mla_w_dkv, mla_w_uk, mla_w_uv, mla_w_o,
                      ffn_w_gate, ffn_w_up, ffn_w_down)
    prm = dict(norm_gains=norm_gains, diff_lambda=diff_lambda, diff_subln=diff_subln, mla_g_q=mla_g_q,
               mla_g_kv=mla_g_kv, ffn_conv_w=ffn_conv_w, ffn_conv_b=ffn_conv_b)
    past = dict(moba=(cache_moba_k, cache_moba_v), diff=(cache_diff_k, cache_diff_v),
                mla=(cache_mla_ckv, cache_mla_krope), conv=state_ffn_conv, page_table=page_table)
    y_prompt, new_p = _trunk(x_prompt, 0, w, prm, None)
    y_sample, new_s = _trunk(x_sample, PAST_LEN, w, prm, past)
    return (y_prompt, y_sample) + new_p + new_s
```

```python
import functools
import math

import jax
import jax.numpy as jnp
from jax import lax
from jax.experimental import pallas as pl
from jax.experimental.pallas import tpu as pltpu

D_MODEL = 2048
PAST_LEN = 8192
PAGE_SIZE = 128
HEAD_DIM = 128
MOBA_HEADS = 8
MOBA_KV_HEADS = 2
MOBA_BLOCK = 256
MOBA_TOPK = 3
DIFF_HEADS = 8
DIFF_KV_HEADS = 2
DIFF_DIM = 64
MLA_HEADS = 16
Q_LORA = 512
KV_LORA = 512
NOPE_DIM = 128
ROPE_DIM = 64
V_DIM = 128
D_FF = 5632
CONV_W = 3
ROPE_THETA = 10000.0
EPS = 1e-6

LANES = 128
SUBLANES = 8
VMEM_LIMIT = 56 * 1024 * 1024
NEG_INF = float("-inf")
BF16 = jnp.bfloat16
F32 = jnp.float32


def _cparams(sem):
    return pltpu.CompilerParams(dimension_semantics=sem, vmem_limit_bytes=VMEM_LIMIT)


def _rms(x, g):
    return x * lax.rsqrt(jnp.mean(x * x, axis=-1, keepdims=True) + EPS) * g


def _dot(a, b):
    return jnp.dot(a, b, preferred_element_type=F32)


def _dot_nt(a, b, precision=None):
    return lax.dot_general(a, b, (((1,), (1,)), ((), ())), preferred_element_type=F32, precision=precision)


def _rope_tile(y, cos, sin, width):
    if width == LANES:
        partner = pltpu.roll(y, LANES // 2, 1)
    else:
        lane = lax.broadcasted_iota(jnp.int32, y.shape, 1)
        half = width // 2
        partner = jnp.where(lane % width < half, pltpu.roll(y, LANES - half, 1), pltpu.roll(y, half, 1))
    return y * cos + partner * sin


def _proj_kernel(*refs, n_w, norm_in, epis, col_chunk):
    it = iter(refs)
    x_ref = next(it)
    g_ref = next(it) if norm_in else None
    w_refs = [next(it) for _ in range(n_w)]
    e_refs = []
    for e in epis:
        if e in ("rope128", "rope64"):
            e_refs.append((next(it), next(it)))
        elif e == "rms":
            e_refs.append((next(it),))
        else:
            e_refs.append(())
    o_refs = [next(it) for _ in range(n_w)]

    x = x_ref[...]
    if norm_in:
        x = _rms(x, g_ref[...])
    xb = x.astype(BF16)
    for w_ref, e, er, o_ref in zip(w_refs, epis, e_refs, o_refs):
        n = w_ref.shape[1]
        if e == "rms":
            o_ref[...] = _rms(_dot(xb, w_ref[...]), er[0][...])
            continue
        for c0 in range(0, n, col_chunk):
            c1 = min(n, c0 + col_chunk)
            y = _dot(xb, w_ref[:, c0:c1])
            if e == "none":
                o_ref[:, c0:c1] = y
            else:
                width = LANES if e == "rope128" else DIFF_DIM
                cos, sin = er[0][...], er[1][...]
                for t0 in range(0, c1 - c0, LANES):
                    o_ref[:, c0 + t0:c0 + t0 + LANES] = _rope_tile(y[:, t0:t0 + LANES], cos, sin, width)


def _proj(x, gain, weights, epis, extras, out_widths, tm):
    m, k = x.shape
    assert m % tm == 0
    row = lambda i: (i, 0)
    fixed = lambda i: (0, 0)
    args, specs = [x], [pl.BlockSpec((tm, k), row)]
    if gain is not None:
        args.append(gain.reshape(1, k))
        specs.append(pl.BlockSpec((1, k), fixed))
    for w in weights:
        args.append(w)
        specs.append(pl.BlockSpec(w.shape, fixed, pipeline_mode=pl.Buffered(1)))
    for e, ex in zip(epis, extras):
        if e in ("rope128", "rope64"):
            for t in ex:
                args.append(t)
                specs.append(pl.BlockSpec((tm, LANES), row))
        elif e == "rms":
            args.append(ex[0].reshape(1, -1))
            specs.append(pl.BlockSpec((1, ex[0].size), fixed))
    out_shape = [jax.ShapeDtypeStruct((m, n), F32) for n in out_widths]
    out_specs = [pl.BlockSpec((tm, n), row) for n in out_widths]
    kern = functools.partial(_proj_kernel, n_w=len(weights), norm_in=gain is not None, epis=tuple(epis),
                             col_chunk=512)
    return pl.pallas_call(kern, grid=(m // tm,), in_specs=specs, out_specs=out_specs, out_shape=out_shape,
                          compiler_params=_cparams(("parallel",)))(*args)


def _outproj_kernel(*refs, n_a):
    a_refs = refs[:n_a]
    w_refs = refs[n_a:2 * n_a]
    g_ref, x_ref, o_ref = refs[2 * n_a:]
    y = _dot(a_refs[0][...].astype(BF16), w_refs[0][...])
    for a_ref, w_ref in zip(a_refs[1:], w_refs[1:]):
        y = y + _dot(a_ref[...].astype(BF16), w_ref[...])
    o_ref[...] = x_ref[...] + _rms(y, g_ref[...])


def _outproj(acts, weights, gain, resid, tm):
    m, d = resid.shape
    row = lambda i: (i, 0)
    fixed = lambda i: (0, 0)
    specs = [pl.BlockSpec((tm, a.shape[1]), row) for a in acts]
    specs += [pl.BlockSpec(w.shape, fixed, pipeline_mode=pl.Buffered(1)) for w in weights]
    specs += [pl.BlockSpec((1, d), fixed), pl.BlockSpec((tm, d), row)]
    return pl.pallas_call(functools.partial(_outproj_kernel, n_a=len(acts)), grid=(m // tm,), in_specs=specs,
                          out_specs=pl.BlockSpec((tm, d), row), out_shape=jax.ShapeDtypeStruct((m, d), F32),
                          compiler_params=_cparams(("parallel",)))(*acts, *weights, gain.reshape(1, d), resid)


def _ffn_kernel(x_ref, g_in_ref, wg_ref, wu_ref, cw_ref, cb_ref, wd_ref, g_out_ref, past_ref,
                y_ref, tail_ref, h_s, acc_s, gbuf_s, carry_s, *, blocks_per_seq):
    i, f = pl.program_id(0), pl.program_id(1)
    n_seq, t_blk, tf = gbuf_s.shape[0], gbuf_s.shape[1] - SUBLANES, gbuf_s.shape[2]
    halo = CONV_W - 1

    @pl.when(f == 0)
    def _():
        h_s[...] = _rms(x_ref[...], g_in_ref[...]).astype(BF16)
        acc_s[...] = jnp.zeros_like(acc_s)

    h = h_s[...]
    g = _dot(h, wg_ref[...])
    u = _dot(h, wu_ref[...])

    if blocks_per_seq == 1:
        gbuf_s[:, 0:SUBLANES, :] = past_ref[...]
    else:
        @pl.when(i % blocks_per_seq == 0)
        def _():
            gbuf_s[:, 0:SUBLANES, :] = past_ref[...]

        @pl.when(i % blocks_per_seq != 0)
        def _():
            gbuf_s[:, 0:SUBLANES, :] = carry_s[f]
    gbuf_s[:, SUBLANES:, :] = g.reshape(n_seq, t_blk, tf)
    if blocks_per_seq > 1:
        carry_s[f] = gbuf_s[:, t_blk:, :]
    tail_ref[...] = gbuf_s[:, t_blk + SUBLANES - halo:, :]

    gc = cb_ref[...].reshape(1, 1, tf)
    for j in range(CONV_W):
        gc = gc + cw_ref[j:j + 1, :].reshape(1, 1, tf) * gbuf_s[:, SUBLANES - halo + j:SUBLANES - halo + j + t_blk, :]
    act = jax.nn.gelu(gc.reshape(n_seq * t_blk, tf), approximate=True) * u
    acc_s[...] += _dot(act.astype(BF16), wd_ref[...])

    @pl.when(f == pl.num_programs(1) - 1)
    def _():
        y_ref[...] = x_ref[...] + _rms(acc_s[...], g_out_ref[...])


def _ffn(x, past8, g_in, wg, wu, cw, cb, wd, g_out, seq_len, tm, tf):
    m, d = x.shape
    n_seq = m // seq_len
    if seq_len >= tm:
        blocks_per_seq, seq_per_blk, t_blk = seq_len // tm, 1, tm
    else:
        blocks_per_seq, seq_per_blk, t_blk = 1, tm // seq_len, seq_len
    n_f = D_FF // tf
    halo = CONV_W - 1
    seq_idx = lambda i, f: (i // blocks_per_seq, 0, f)
    in_specs = [
        pl.BlockSpec((tm, d), lambda i, f: (i, 0)),
        pl.BlockSpec((1, d), lambda i, f: (0, 0)),
        pl.BlockSpec((d, tf), lambda i, f: (0, f)),
        pl.BlockSpec((d, tf), lambda i, f: (0, f)),
        pl.BlockSpec((CONV_W, tf), lambda i, f: (0, f)),
        pl.BlockSpec((1, tf), lambda i, f: (0, f)),
        pl.BlockSpec((tf, d), lambda i, f: (f, 0)),
        pl.BlockSpec((1, d), lambda i, f: (0, 0)),
        pl.BlockSpec((seq_per_blk, SUBLANES, tf), seq_idx),
    ]
    out_specs = [pl.BlockSpec((tm, d), lambda i, f: (i, 0)),
                 pl.BlockSpec((seq_per_blk, halo, tf), lambda i, f: (i, 0, f))]
    out_shape = [jax.ShapeDtypeStruct((m, d), F32),
                 jax.ShapeDtypeStruct((m // tm * seq_per_blk, halo, D_FF), F32)]
    scratch = [pltpu.VMEM((tm, d), BF16), pltpu.VMEM((tm, d), F32),
               pltpu.VMEM((seq_per_blk, t_blk + SUBLANES, tf), F32),
               pltpu.VMEM((n_f, seq_per_blk, SUBLANES, tf), F32)]
    kern = functools.partial(_ffn_kernel, blocks_per_seq=blocks_per_seq)
    y, tails = pl.pallas_call(kern, grid=(m // tm, n_f), in_specs=in_specs, out_specs=out_specs,
                              out_shape=out_shape, scratch_shapes=scratch,
                              compiler_params=_cparams(("arbitrary", "arbitrary")))(
        x, g_in.reshape(1, d), wg, wu, cw, cb.reshape(1, D_FF), wd, g_out.reshape(1, d), past8)
    return y, tails[blocks_per_seq - 1::blocks_per_seq]


def _weighted_values(p, values):
    if not isinstance(values, (list, tuple)):
        return _dot(p, values)
    r = p.shape[0] // len(values)
    return jnp.concatenate([_dot(p[i * r:(i + 1) * r], v) for i, v in enumerate(values)], axis=0)


def _softmax_init(s, values, m_s, l_s, acc_s):
    m = jnp.max(s, axis=-1, keepdims=True)
    p = jnp.exp(s - m)
    m_s[...] = m
    l_s[...] = jnp.sum(p, axis=-1, keepdims=True)
    acc_s[...] = _weighted_values(p.astype(BF16), values)


def _softmax_update(s, values, m_s, l_s, acc_s):
    m_old = m_s[...]
    m_new = jnp.maximum(m_old, jnp.max(s, axis=-1, keepdims=True))
    alpha = jnp.exp(m_old - m_new)
    p = jnp.exp(s - m_new)
    m_s[...] = m_new
    l_s[...] = alpha * l_s[...] + jnp.sum(p, axis=-1, keepdims=True)
    acc_s[...] = alpha * acc_s[...] + _weighted_values(p.astype(BF16), values)


def _topk_mask(gate, valid, k, axis=1):
    lane = lax.broadcasted_iota(jnp.int32, gate.shape, axis)
    big = jnp.int32(gate.shape[axis])
    sel = jnp.zeros(gate.shape, jnp.bool_)
    rem = valid
    for _ in range(k):
        gm = jnp.where(rem, gate, NEG_INF)
        top = jnp.max(gm, axis=axis, keepdims=True)
        idx = jnp.min(jnp.where(rem & (gm == top), lane, big), axis=axis, keepdims=True)
        pick = lane == idx
        sel = sel | pick
        rem = rem & jnp.logical_not(pick)
    return sel


def _causal_tile(rows, cols, period):
    r = lax.broadcasted_iota(jnp.int32, (rows, cols), 0) % period
    c = lax.broadcasted_iota(jnp.int32, (rows, cols), 1)
    return c <= r


def _diff_lambda(lam_ref, lam_init):
    lv = lam_ref[...]
    d1 = jnp.sum(lv[0:1] * lv[1:2], axis=-1, keepdims=True)
    d2 = jnp.sum(lv[2:3] * lv[3:4], axis=-1, keepdims=True)
    return jnp.exp(d1) - jnp.exp(d2) + lam_init


def _split_maps(q2):
    lane = lax.broadcasted_iota(jnp.int32, q2.shape, 1)
    lo = jnp.where(lane < DIFF_DIM, q2, 0.0)
    hi = jnp.where(lane >= DIFF_DIM, q2, 0.0)
    return jnp.concatenate([lo, hi], axis=0)


TQ = MOBA_BLOCK
DIFF_HEAD_GROUP = 2
MLA_HEAD_GROUP = 4
LOG2_E = math.log2(math.e)


def _lane_halves_max(s):
    out = s[:, :LANES]
    for c0 in range(LANES, s.shape[1], LANES):
        out = jnp.maximum(out, s[:, c0:c0 + LANES])
    return out


def _lane_halves_sum(p):
    out = p[:, :LANES]
    for c0 in range(LANES, p.shape[1], LANES):
        out = out + p[:, c0:c0 + LANES]
    return out


def _two_pass_attend(qk, values, own_mask, past_bias, qi, scale, mx_s, l_s, acc_s):
    def raw_scores(j, k0):
        s = qk(k0)
        if past_bias is not None:
            s = s + jnp.concatenate([past_bias(j)] * (TQ // LANES), axis=1)
        return s

    own0 = pl.multiple_of(qi * TQ, TQ)
    s_own = jnp.where(own_mask, qk(own0), NEG_INF)
    mx_s[...] = _lane_halves_max(s_own)

    def pass1(j, carry):
        mx_s[...] = jnp.maximum(mx_s[...], _lane_halves_max(raw_scores(j, pl.multiple_of(j * TQ, TQ))))
        return carry

    lax.fori_loop(0, qi, pass1, 0)
    mx_s[...] = jnp.broadcast_to(jnp.max(mx_s[...], axis=-1, keepdims=True), mx_s.shape)

    def probs(s):
        m = jnp.concatenate([mx_s[...]] * (TQ // LANES), axis=1)
        return jnp.exp2((s - m) * (scale * LOG2_E))

    p = probs(s_own)
    l_s[...] = _lane_halves_sum(p)
    acc_s[...] = _weighted_values(p.astype(BF16), values(own0))

    def pass2(j, carry):
        k0 = pl.multiple_of(j * TQ, TQ)
        p = probs(raw_scores(j, k0))
        l_s[...] += _lane_halves_sum(p)
        acc_s[...] += _weighted_values(p.astype(BF16), values(k0))
        return carry

    lax.fori_loop(0, qi, pass2, 0)
    return acc_s[...] / jnp.sum(l_s[...], axis=-1, keepdims=True)


def _moba_prefill_kernel(q_ref, k_ref, v_ref, o_ref, kb_s, vb_s, bias_s, mx_s, l_s, acc_s):
    qi = pl.program_id(2)
    t = k_ref.shape[0]
    nb = t // MOBA_BLOCK
    n_sel = max(1, min(MOBA_TOPK, nb - 1))
    scale = HEAD_DIM ** -0.5
    g_per = MOBA_HEADS // MOBA_KV_HEADS

    @pl.when(qi == 0)
    def _():
        kb_s[...] = k_ref[...].astype(BF16)
        vb_s[...] = v_ref[...].astype(BF16)

    kmean = jnp.mean(k_ref[...].reshape(nb, MOBA_BLOCK, HEAD_DIM), axis=1)
    rows = g_per * TQ
    qf = _rows_by_head(q_ref, range(g_per), HEAD_DIM)
    qb = qf.astype(BF16)
    blk = lax.broadcasted_iota(jnp.int32, (nb, rows), 0)
    gate = _dot_nt(kmean, qf, lax.Precision.HIGHEST)
    sel = _topk_mask(gate, blk < qi, n_sel, axis=0).astype(F32)
    spread = (lax.broadcasted_iota(jnp.int32, (nb, nb * LANES), 1) // LANES
              == lax.broadcasted_iota(jnp.int32, (nb, nb * LANES), 0)).astype(F32)
    keep = lax.dot_general(sel, spread, (((0,), (0,)), ((), ())), preferred_element_type=F32)
    for j in range(nb):
        bias_s[j] = jnp.where(keep[:, j * LANES:(j + 1) * LANES] > 0.5, 0.0, NEG_INF)
    o = _two_pass_attend(lambda k0: _dot_nt(qb, kb_s[pl.ds(k0, TQ), :]), lambda k0: vb_s[pl.ds(k0, TQ), :],
                         _causal_tile(rows, TQ, TQ), lambda j: bias_s[j], qi, scale, mx_s, l_s, acc_s)
    for g in range(g_per):
        o_ref[:, g * HEAD_DIM:(g + 1) * HEAD_DIM] = o[g * TQ:(g + 1) * TQ]


def _moba_prefill(mq, mk, mv, n_batch, t):
    m = mq.shape[0]
    nq = t // TQ
    g_w = MOBA_HEADS // MOBA_KV_HEADS * HEAD_DIM
    q_spec = pl.BlockSpec((TQ, g_w), lambda b, k, i: (b * nq + i, k))
    kv_spec = pl.BlockSpec((t, HEAD_DIM), lambda b, k, i: (b, k))
    rows = MOBA_HEADS // MOBA_KV_HEADS * TQ
    scratch = [pltpu.VMEM((t, HEAD_DIM), BF16), pltpu.VMEM((t, HEAD_DIM), BF16),
               pltpu.VMEM((t // MOBA_BLOCK, rows, LANES), F32),
               pltpu.VMEM((rows, LANES), F32), pltpu.VMEM((rows, LANES), F32), pltpu.VMEM((rows, HEAD_DIM), F32)]
    return pl.pallas_call(_moba_prefill_kernel, grid=(n_batch, MOBA_KV_HEADS, nq),
                          in_specs=[q_spec, kv_spec, kv_spec], out_specs=q_spec,
                          out_shape=jax.ShapeDtypeStruct((m, MOBA_HEADS * HEAD_DIM), F32), scratch_shapes=scratch,
                          compiler_params=_cparams(("parallel", "parallel", "arbitrary")))(mq, mk, mv)


def _diff_finish(o2, lam, sub_g, lam_init, rows):
    o = o2[:rows] - lam * o2[rows:]
    return _rms(o, sub_g) * (1.0 - lam_init)


def _diff_prefill_kernel(q_ref, k_ref, v_ref, lam_ref, subg_ref, o_ref, kb_s, vb_s, mx_s, l_s, acc_s, *, lam_init):
    qi = pl.program_id(2)
    scale = DIFF_DIM ** -0.5
    g_per = DIFF_HEADS // DIFF_KV_HEADS
    width = 2 * DIFF_DIM

    @pl.when(qi == 0)
    def _():
        kb_s[...] = k_ref[...].astype(BF16)
        vb_s[...] = v_ref[...].astype(BF16)

    lam = _diff_lambda(lam_ref, lam_init)
    grp = mx_s.shape[0] // (2 * TQ)
    tril = _causal_tile(grp * 2 * TQ, TQ, TQ)
    for g0 in range(0, g_per, grp):
        qb = jnp.concatenate([_split_maps(q_ref[:, g * width:(g + 1) * width]) for g in range(g0, g0 + grp)],
                             axis=0).astype(BF16)
        o2 = _two_pass_attend(lambda k0: _dot_nt(qb, kb_s[pl.ds(k0, TQ), :]), lambda k0: vb_s[pl.ds(k0, TQ), :],
                              tril, None, qi, scale, mx_s, l_s, acc_s)
        for i in range(grp):
            g = g0 + i
            o_ref[:, g * width:(g + 1) * width] = _diff_finish(
                o2[i * 2 * TQ:(i + 1) * 2 * TQ], lam, subg_ref[...], lam_init, TQ)


def _diff_prefill(dq, dk, dv, lam_vec, sub_g, lam_init, n_batch, t):
    m = dq.shape[0]
    nq = t // TQ
    width = 2 * DIFF_DIM
    g_w = DIFF_HEADS // DIFF_KV_HEADS * width
    q_spec = pl.BlockSpec((TQ, g_w), lambda b, k, i: (b * nq + i, k))
    kv_spec = pl.BlockSpec((t, width), lambda b, k, i: (b, k))
    fixed = lambda b, k, i: (0, 0)
    rows = DIFF_HEAD_GROUP * 2 * TQ
    scratch = [pltpu.VMEM((t, width), BF16), pltpu.VMEM((t, width), BF16),
               pltpu.VMEM((rows, LANES), F32), pltpu.VMEM((rows, LANES), F32), pltpu.VMEM((rows, width), F32)]
    kern = functools.partial(_diff_prefill_kernel, lam_init=lam_init)
    return pl.pallas_call(kern, grid=(n_batch, DIFF_KV_HEADS, nq),
                          in_specs=[q_spec, kv_spec, kv_spec, pl.BlockSpec((4, DIFF_DIM), fixed),
                                    pl.BlockSpec((1, width), fixed)],
                          out_specs=q_spec, out_shape=jax.ShapeDtypeStruct((m, DIFF_HEADS * width), F32),
                          scratch_shapes=scratch,
                          compiler_params=_cparams(("parallel", "parallel", "arbitrary")))(
        dq, dk, dv, lam_vec, sub_g.reshape(1, width))


def _mla_prefill_kernel(qn_ref, qr_ref, ckv_ref, kr2_ref, wuk_ref, wuv_ref, o_ref, kx_s, vx_s, mx_s, l_s, acc_s):
    qi = pl.program_id(2)
    grp = kx_s.shape[0]
    scale = (NOPE_DIM + ROPE_DIM) ** -0.5

    @pl.when(qi == 0)
    def _():
        cb = ckv_ref[...].astype(BF16)
        kr = kr2_ref[...].astype(BF16)
        for i in range(grp):
            kx_s[i, :, :NOPE_DIM] = _dot_nt(cb, wuk_ref[i]).astype(BF16)
            kx_s[i, :, NOPE_DIM:] = kr
            vx_s[i] = _dot(cb, wuv_ref[i]).astype(BF16)

    lane = lax.broadcasted_iota(jnp.int32, (TQ, LANES), 1)
    q_heads = []
    for i in range(grp):
        pair = qr_ref[:, (i // 2) * LANES:(i // 2 + 1) * LANES]
        in_head = (lane < ROPE_DIM) if i % 2 == 0 else (lane >= ROPE_DIM)
        q_r = jnp.where(in_head, pair, 0.0)
        q_heads.append(jnp.concatenate([qn_ref[:, i * NOPE_DIM:(i + 1) * NOPE_DIM], q_r], axis=1).astype(BF16))

    def qk(k0):
        return jnp.concatenate([_dot_nt(q_heads[i], kx_s[i, pl.ds(k0, TQ), :]) for i in range(grp)], axis=0)

    o = _two_pass_attend(qk, lambda k0: [vx_s[i, pl.ds(k0, TQ), :] for i in range(grp)],
                         _causal_tile(grp * TQ, TQ, TQ), None, qi, scale, mx_s, l_s, acc_s)
    for i in range(grp):
        o_ref[:, i * V_DIM:(i + 1) * V_DIM] = o[i * TQ:(i + 1) * TQ]


def _mla_prefill(q_nope, q_rope, ckv, kr2, wuk_t, wuv_t, n_batch, t):
    m = q_nope.shape[0]
    nq = t // TQ
    grp = MLA_HEAD_GROUP
    row = lambda b, g, i: (b * nq + i, g)
    seq = lambda b, g, i: (b, 0)
    head = lambda b, g, i: (g, 0, 0)
    scratch = [pltpu.VMEM((grp, t, NOPE_DIM + LANES), BF16), pltpu.VMEM((grp, t, V_DIM), BF16),
               pltpu.VMEM((grp * TQ, LANES), F32), pltpu.VMEM((grp * TQ, LANES), F32),
               pltpu.VMEM((grp * TQ, V_DIM), F32)]
    in_specs = [pl.BlockSpec((TQ, grp * NOPE_DIM), row), pl.BlockSpec((TQ, grp * ROPE_DIM), row),
                pl.BlockSpec((t, KV_LORA), seq), pl.BlockSpec((t, LANES), seq),
                pl.BlockSpec((grp,) + wuk_t.shape[1:], head), pl.BlockSpec((grp,) + wuv_t.shape[1:], head)]
    return pl.pallas_call(_mla_prefill_kernel, grid=(n_batch, MLA_HEADS // grp, nq), in_specs=in_specs,
                          out_specs=pl.BlockSpec((TQ, grp * V_DIM), row),
                          out_shape=jax.ShapeDtypeStruct((m, MLA_HEADS * V_DIM), F32), scratch_shapes=scratch,
                          compiler_params=_cparams(("parallel", "parallel", "arbitrary")))(
        q_nope, q_rope, ckv, kr2, wuk_t, wuv_t)


PAGES_PER_STEP = 16
N_PAGES = PAST_LEN // PAGE_SIZE


def _page_specs(rows, cols, pages_per_step):
    def spec(u):
        return pl.BlockSpec((None, rows, cols),
                            lambda s, c, pt: (pt[s * N_PAGES + c * pages_per_step + u], 0, 0))
    return [spec(u) for u in range(pages_per_step)]


def _head_rows(refs, kvh, n_kvh):
    return jnp.concatenate([r[pl.ds(kvh, PAGE_SIZE, stride=n_kvh), :] for r in refs], axis=0)


def _rows_by_head(ref, heads, width):
    return jnp.concatenate([ref[:, h * width:(h + 1) * width] for h in heads], axis=0)


def _moba_decode_kernel(pt_ref, q_ref, kn_ref, vn_ref, *refs, pages_per_step, t_new):
    p = pages_per_step
    k_refs, v_refs = refs[:p], refs[p:2 * p]
    o_ref, m_s, l_s, o_s, km_s = refs[2 * p:]
    c = pl.program_id(1)
    scale = HEAD_DIM ** -0.5
    g_per = MOBA_HEADS // MOBA_KV_HEADS
    rows = g_per * t_new
    pages_per_blk = MOBA_BLOCK // PAGE_SIZE
    blk_per_step = p // pages_per_blk
    n_past_blk = PAST_LEN // MOBA_BLOCK
    n_sel = max(1, min(MOBA_TOPK, n_past_blk))

    lane = lax.broadcasted_iota(jnp.int32, (rows, LANES), 1)

    @pl.when(c == 0)
    def _():
        m_s[...] = jnp.zeros_like(m_s)
        l_s[...] = jnp.zeros_like(l_s)

    for kvh in range(MOBA_KV_HEADS):
        q_rows = _rows_by_head(q_ref, range(kvh * g_per, (kvh + 1) * g_per), HEAD_DIM)
        qb = q_rows.astype(BF16)
        kf = _head_rows(k_refs, kvh, MOBA_KV_HEADS)
        vb = _head_rows(v_refs, kvh, MOBA_KV_HEADS).astype(BF16)
        s_all = _dot_nt(qb, kf.astype(BF16)) * scale
        m_mat, l_mat = m_s[kvh], l_s[kvh]
        for b in range(blk_per_step):
            k0, k1 = b * MOBA_BLOCK, (b + 1) * MOBA_BLOCK
            s = s_all[:, k0:k1]
            m = jnp.max(s, axis=-1, keepdims=True)
            e = jnp.exp(s - m)
            blk = c * blk_per_step + b
            m_mat = jnp.where(lane == blk, m, m_mat)
            l_mat = jnp.where(lane == blk, jnp.sum(e, axis=-1, keepdims=True), l_mat)
            o_s[kvh, blk] = _dot(e.astype(BF16), vb[k0:k1])
            km_s[kvh, pl.ds(blk, 1), :] = jnp.mean(kf[k0:k1], axis=0, keepdims=True)
        m_s[kvh] = m_mat
        l_s[kvh] = l_mat

    @pl.when(c == pl.num_programs(1) - 1)
    def _():
        causal = _causal_tile(MOBA_KV_HEADS * rows, t_new, t_new)
        q_k = [_rows_by_head(q_ref, range(k * g_per, (k + 1) * g_per), HEAD_DIM) for k in range(MOBA_KV_HEADS)]
        gate = jnp.concatenate([_dot_nt(q_k[k], km_s[k], lax.Precision.HIGHEST) for k in range(MOBA_KV_HEADS)],
                               axis=0)
        sel = _topk_mask(gate, jnp.ones(gate.shape, jnp.bool_), n_sel)
        s_own = jnp.concatenate(
            [_dot_nt(q_k[k].astype(BF16), kn_ref[:, k * HEAD_DIM:(k + 1) * HEAD_DIM].astype(BF16))
             for k in range(MOBA_KV_HEADS)], axis=0) * scale
        s_own = jnp.where(causal, s_own, NEG_INF)
        m_blk = jnp.concatenate([m_s[k] for k in range(MOBA_KV_HEADS)], axis=0)[:, :n_past_blk]
        l_blk = jnp.concatenate([l_s[k] for k in range(MOBA_KV_HEADS)], axis=0)[:, :n_past_blk]
        m_fin = jnp.maximum(jnp.max(s_own, axis=-1, keepdims=True),
                            jnp.max(jnp.where(sel, m_blk, NEG_INF), axis=-1, keepdims=True))
        e_own = jnp.exp(s_own - m_fin)
        w = jnp.exp(jnp.where(sel, m_blk - m_fin, NEG_INF))
        l_fin = jnp.sum(e_own, axis=-1, keepdims=True) + jnp.sum(w * l_blk, axis=-1, keepdims=True)
        for k in range(MOBA_KV_HEADS):
            r0, r1 = k * rows, (k + 1) * rows
            parts = [_dot(e_own[r0:r1].astype(BF16), vn_ref[:, k * HEAD_DIM:(k + 1) * HEAD_DIM].astype(BF16)),
                     jnp.zeros((rows, HEAD_DIM), F32)]
            for b in range(n_past_blk):
                parts[b % 2] = parts[b % 2] + w[r0:r1, b:b + 1] * o_s[k, b]
            o = (parts[0] + parts[1]) / l_fin[r0:r1]
            for i in range(g_per):
                h = k * g_per + i
                o_ref[:, h * HEAD_DIM:(h + 1) * HEAD_DIM] = o[i * t_new:(i + 1) * t_new]


def _moba_decode(mq, mk, mv, pool_k, pool_v, page_table, t_new):
    n_seq = mq.shape[0] // t_new
    p = PAGES_PER_STEP
    kv_w = MOBA_KV_HEADS * HEAD_DIM
    g_per = MOBA_HEADS // MOBA_KV_HEADS
    rows = g_per * t_new
    n_blk = PAST_LEN // MOBA_BLOCK
    seq = lambda s, c, pt: (s, 0)
    page = _page_specs(PAGE_SIZE * MOBA_KV_HEADS, HEAD_DIM, p)
    in_specs = [pl.BlockSpec((t_new, MOBA_HEADS * HEAD_DIM), seq), pl.BlockSpec((t_new, kv_w), seq),
                pl.BlockSpec((t_new, kv_w), seq)] + page + page
    assert n_blk <= LANES
    scratch = [pltpu.VMEM((MOBA_KV_HEADS, rows, LANES), F32), pltpu.VMEM((MOBA_KV_HEADS, rows, LANES), F32),
               pltpu.VMEM((MOBA_KV_HEADS, n_blk, rows, HEAD_DIM), F32), pltpu.VMEM((MOBA_KV_HEADS, n_blk, HEAD_DIM), F32)]
    grid_spec = pltpu.PrefetchScalarGridSpec(
        num_scalar_prefetch=1, grid=(n_seq, N_PAGES // p), in_specs=in_specs,
        out_specs=pl.BlockSpec((t_new, MOBA_HEADS * HEAD_DIM), seq), scratch_shapes=scratch)
    kern = functools.partial(_moba_decode_kernel, pages_per_step=p, t_new=t_new)
    return pl.pallas_call(kern, grid_spec=grid_spec, out_shape=jax.ShapeDtypeStruct(mq.shape, F32),
                          compiler_params=_cparams(("parallel", "arbitrary")))(
        page_table.reshape(-1), mq, mk, mv, *([pool_k] * p), *([pool_v] * p))


def _diff_decode_kernel(pt_ref, q_ref, kn_ref, vn_ref, lam_ref, subg_ref, *refs, pages_per_step, t_new, lam_init):
    p = pages_per_step
    k_refs, v_refs = refs[:p], refs[p:2 * p]
    o_ref, m_s, l_s, acc_s = refs[2 * p:]
    c = pl.program_id(1)
    scale = DIFF_DIM ** -0.5
    g_per = DIFF_HEADS // DIFF_KV_HEADS
    width = 2 * DIFF_DIM
    rows = g_per * t_new
    kvhs = range(DIFF_KV_HEADS)
    qb = [_split_maps(_rows_by_head(q_ref, range(k * g_per, (k + 1) * g_per), width)).astype(BF16) for k in kvhs]

    @pl.when(c == 0)
    def _():
        causal = _causal_tile(DIFF_KV_HEADS * 2 * rows, t_new, t_new)
        s_own = jnp.concatenate([_dot_nt(qb[k], kn_ref[:, k * width:(k + 1) * width].astype(BF16)) for k in kvhs],
                                axis=0) * scale
        _softmax_init(jnp.where(causal, s_own, NEG_INF),
                      [vn_ref[:, k * width:(k + 1) * width].astype(BF16) for k in kvhs], m_s, l_s, acc_s)

    s = jnp.concatenate(
        [_dot(qb[k], jnp.concatenate([r[k * width:(k + 1) * width, :] for r in k_refs], axis=1).astype(BF16))
         for k in kvhs], axis=0) * scale
    _softmax_update(s, [_head_rows(v_refs, k, DIFF_KV_HEADS).astype(BF16) for k in kvhs], m_s, l_s, acc_s)

    @pl.when(c == pl.num_programs(1) - 1)
    def _():
        lam = _diff_lambda(lam_ref, lam_init)
        o2 = acc_s[...] / l_s[...]
        for k in kvhs:
            o = _diff_finish(o2[k * 2 * rows:(k + 1) * 2 * rows], lam, subg_ref[...], lam_init, rows)
            for i in range(g_per):
                h = k * g_per + i
                o_ref[:, h * width:(h + 1) * width] = o[i * t_new:(i + 1) * t_new]


def _diff_decode(dq, dk, dv, pool_k, pool_v, page_table, lam_vec, sub_g, lam_init, t_new):
    n_seq = dq.shape[0] // t_new
    p = PAGES_PER_STEP
    width = 2 * DIFF_DIM
    kv_w = DIFF_KV_HEADS * width
    rows2 = 2 * (DIFF_HEADS // DIFF_KV_HEADS) * t_new
    seq = lambda s, c, pt: (s, 0)
    fixed = lambda s, c, pt: (0, 0)
    in_specs = [pl.BlockSpec((t_new, DIFF_HEADS * width), seq), pl.BlockSpec((t_new, kv_w), seq),
                pl.BlockSpec((t_new, kv_w), seq), pl.BlockSpec((4, DIFF_DIM), fixed),
                pl.BlockSpec((1, width), fixed)]
    in_specs += _page_specs(kv_w, PAGE_SIZE, p) + _page_specs(PAGE_SIZE * DIFF_KV_HEADS, width, p)
    scratch = [pltpu.VMEM((DIFF_KV_HEADS * rows2, 1), F32), pltpu.VMEM((DIFF_KV_HEADS * rows2, 1), F32),
               pltpu.VMEM((DIFF_KV_HEADS * rows2, width), F32)]
    grid_spec = pltpu.PrefetchScalarGridSpec(
        num_scalar_prefetch=1, grid=(n_seq, N_PAGES // p), in_specs=in_specs,
        out_specs=pl.BlockSpec((t_new, DIFF_HEADS * width), seq), scratch_shapes=scratch)
    kern = functools.partial(_diff_decode_kernel, pages_per_step=p, t_new=t_new, lam_init=lam_init)
    return pl.pallas_call(kern, grid_spec=grid_spec, out_shape=jax.ShapeDtypeStruct(dq.shape, F32),
                          compiler_params=_cparams(("parallel", "arbitrary")))(
        page_table.reshape(-1), dq, dk, dv, lam_vec, sub_g.reshape(1, width), *([pool_k] * p), *([pool_v] * p))


def _mla_decode_kernel(pt_ref, qn_ref, qr_ref, cn_ref, krn_ref, wuk_ref, wuv_ref, *refs, pages_per_step, t_new):
    p = pages_per_step
    c_refs, r_refs = refs[:p], refs[p:2 * p]
    o_ref, qlat_s, m_s, l_s, acc_s = refs[2 * p:]
    c = pl.program_id(1)
    scale = (NOPE_DIM + ROPE_DIM) ** -0.5
    rows = MLA_HEADS * t_new

    @pl.when(c == 0)
    def _():
        for h in range(MLA_HEADS):
            qlat_s[h * t_new:(h + 1) * t_new, :] = _dot(
                qn_ref[:, h * NOPE_DIM:(h + 1) * NOPE_DIM].astype(BF16), wuk_ref[h])
        q_lat = qlat_s[...].astype(BF16)
        causal = _causal_tile(rows, t_new, t_new)
        cn = cn_ref[...].astype(BF16)
        s_own = (_dot_nt(q_lat, cn) + _dot_nt(qr_ref[...].astype(BF16), krn_ref[...].astype(BF16))) * scale
        _softmax_init(jnp.where(causal, s_own, NEG_INF), cn, m_s, l_s, acc_s)

    q_lat = qlat_s[...].astype(BF16)
    ckv = jnp.concatenate([r[...] for r in c_refs], axis=0).astype(BF16)
    kr_t = jnp.concatenate([r[...] for r in r_refs], axis=1).astype(BF16)
    s = (_dot_nt(q_lat, ckv) + _dot(qr_ref[...].astype(BF16), kr_t)) * scale
    _softmax_update(s, ckv, m_s, l_s, acc_s)

    @pl.when(c == pl.num_programs(1) - 1)
    def _():
        o_lat = (acc_s[...] / l_s[...]).astype(BF16)
        for h in range(MLA_HEADS):
            o_ref[:, h * V_DIM:(h + 1) * V_DIM] = _dot(o_lat[h * t_new:(h + 1) * t_new], wuv_ref[h])


def _mla_decode(q_nope, q_rope_rows, ckv_new, kr_new, pool_c, pool_r, page_table, wuk_t, wuv_t, t_new):
    n_seq = q_nope.shape[0] // t_new
    p = PAGES_PER_STEP
    rows = MLA_HEADS * t_new
    seq = lambda s, c, pt: (s, 0)
    fixed3 = lambda s, c, pt: (0, 0, 0)
    in_specs = [pl.BlockSpec((t_new, MLA_HEADS * NOPE_DIM), seq),
                pl.BlockSpec((None, rows, ROPE_DIM), lambda s, c, pt: (s, 0, 0)),
                pl.BlockSpec((t_new, KV_LORA), seq), pl.BlockSpec((t_new, ROPE_DIM), seq),
                pl.BlockSpec(wuk_t.shape, fixed3, pipeline_mode=pl.Buffered(1)),
                pl.BlockSpec(wuv_t.shape, fixed3, pipeline_mode=pl.Buffered(1))]
    in_specs += _page_specs(PAGE_SIZE, KV_LORA, p) + _page_specs(ROPE_DIM, PAGE_SIZE, p)
    scratch = [pltpu.VMEM((rows, KV_LORA), F32), pltpu.VMEM((rows, 1), F32), pltpu.VMEM((rows, 1), F32),
               pltpu.VMEM((rows, KV_LORA), F32)]
    grid_spec = pltpu.PrefetchScalarGridSpec(
        num_scalar_prefetch=1, grid=(n_seq, N_PAGES // p), in_specs=in_specs,
        out_specs=pl.BlockSpec((t_new, MLA_HEADS * V_DIM), seq), scratch_shapes=scratch)
    kern = functools.partial(_mla_decode_kernel, pages_per_step=p, t_new=t_new)
    return pl.pallas_call(kern, grid_spec=grid_spec,
                          out_shape=jax.ShapeDtypeStruct((n_seq * t_new, MLA_HEADS * V_DIM), F32),
                          compiler_params=_cparams(("parallel", "arbitrary")))(
        page_table.reshape(-1), q_nope, q_rope_rows, ckv_new, kr_new, wuk_t, wuv_t, *([pool_c] * p), *([pool_r] * p))


ROW_TILE = 512
FF_TILE = 512


def _rope_tables(pos, width):
    inv = 1.0 / (ROPE_THETA ** (jnp.arange(0, width, 2, dtype=F32) / width))
    ang = pos.astype(F32)[:, None] * inv
    cos, sin = jnp.cos(ang), jnp.sin(ang)
    reps = LANES // width
    return jnp.tile(jnp.concatenate([cos, cos], axis=1), (1, reps)), jnp.tile(jnp.concatenate([-sin, sin], axis=1), (1, reps))


def _prep_weights(w_in_ab, w_out_ab, mla_w_dq, mla_w_uq, mla_w_dkv, mla_w_uk, mla_w_uv, mla_w_o,
                  ffn_w_gate, ffn_w_up, ffn_w_down):
    bf = lambda w: w.astype(BF16)
    cols = (MOBA_HEADS * HEAD_DIM, MOBA_KV_HEADS * HEAD_DIM, MOBA_KV_HEADS * HEAD_DIM,
            DIFF_HEADS * 2 * DIFF_DIM, DIFF_KV_HEADS * 2 * DIFF_DIM, DIFF_KV_HEADS * 2 * DIFF_DIM)
    offs = [sum(cols[:i]) for i in range(len(cols) + 1)]
    w_in = bf(w_in_ab[0])
    w_out = bf(w_out_ab[0])
    uq = bf(mla_w_uq[0]).reshape(-1, MLA_HEADS, NOPE_DIM + ROPE_DIM)
    dkv = bf(mla_w_dkv[0])
    return dict(
        ab_in=[w_in[:, offs[i]:offs[i + 1]] for i in range(len(cols))],
        ab_out=[w_out[:MOBA_HEADS * HEAD_DIM], w_out[MOBA_HEADS * HEAD_DIM:]],
        dq=bf(mla_w_dq[0]),
        uq_nope=uq[:, :, :NOPE_DIM].reshape(-1, MLA_HEADS * NOPE_DIM),
        uq_rope=uq[:, :, NOPE_DIM:].reshape(-1, MLA_HEADS * ROPE_DIM),
        dkv_c=dkv[:, :KV_LORA],
        dkv_r2=jnp.concatenate([dkv[:, KV_LORA:], dkv[:, KV_LORA:]], axis=1),
        uk_t=bf(mla_w_uk[0]).transpose(1, 2, 0),
        uv_t=bf(mla_w_uv[0]).transpose(1, 0, 2),
        o=bf(mla_w_o[0]),
        gate=bf(ffn_w_gate), up=bf(ffn_w_up), down=bf(ffn_w_down),
    )


def _trunk(x3, pos0, w, prm, past):
    n_seq, t, d = x3.shape
    x = x3.reshape(n_seq * t, d)
    pos = jnp.tile(pos0 + jnp.arange(t, dtype=jnp.int32), n_seq)
    rope128 = _rope_tables(pos, HEAD_DIM)
    rope64 = _rope_tables(pos, DIFF_DIM)
    gains = prm["norm_gains"]
    lam_init0 = 0.8 - 0.6 * math.exp(-0.3 * 0)
    halo = CONV_W - 1

    def ffn(xin, layer):
        if past is None:
            past8 = jnp.zeros((n_seq, SUBLANES, D_FF), F32)
        else:
            past8 = jnp.pad(past["conv"][layer], ((0, 0), (SUBLANES - halo, 0), (0, 0)))
        return _ffn(xin, past8, gains[layer, 2], w["gate"][layer], w["up"][layer], prm["ffn_conv_w"][layer],
                    prm["ffn_conv_b"][layer], w["down"][layer], gains[layer, 3], t, ROW_TILE, FF_TILE)

    mq, mk, mv, dq, dk, dv = _proj(
        x, gains[0, 0], w["ab_in"], ["rope128", "rope128", "none", "rope64", "rope64", "none"],
        [rope128, rope128, (), rope64, rope64, ()], [wi.shape[1] for wi in w["ab_in"]], ROW_TILE // 2)
    if past is None:
        o_m = _moba_prefill(mq, mk, mv, n_seq, t)
        o_d = _diff_prefill(dq, dk, dv, prm["diff_lambda"][0], prm["diff_subln"][0], lam_init0, n_seq, t)
    else:
        pt = past["page_table"]
        by_row = lambda pool: pool[0].reshape(pool.shape[1], -1, LANES)
        by_feature = lambda pool: pool[0].reshape(pool.shape[1], PAGE_SIZE, -1).transpose(0, 2, 1)
        o_m = _moba_decode(mq, mk, mv, by_row(past["moba"][0]), by_row(past["moba"][1]), pt, t)
        o_d = _diff_decode(dq, dk, dv, by_feature(past["diff"][0]), by_row(past["diff"][1]), pt,
                           prm["diff_lambda"][0], prm["diff_subln"][0], lam_init0, t)
    x = _outproj([o_m, o_d], w["ab_out"], gains[0, 1], x, ROW_TILE)
    x, conv0 = ffn(x, 0)

    cq, ckv, kr2 = _proj(x, gains[1, 0], [w["dq"], w["dkv_c"], w["dkv_r2"]], ["rms", "rms", "rope64"],
                         [(prm["mla_g_q"][0],), (prm["mla_g_kv"][0],), rope64], [Q_LORA, KV_LORA, LANES], ROW_TILE)
    q_nope, q_rope = _proj(cq, None, [w["uq_nope"], w["uq_rope"]], ["none", "rope64"], [(), rope64],
                           [MLA_HEADS * NOPE_DIM, MLA_HEADS * ROPE_DIM], ROW_TILE)
    kr = kr2[:, :ROPE_DIM]
    if past is None:
        o = _mla_prefill(q_nope, q_rope, ckv, kr2, w["uk_t"], w["uv_t"], n_seq, t)
    else:
        q_rope_rows = q_rope.reshape(n_seq, t, MLA_HEADS, ROPE_DIM).transpose(0, 2, 1, 3).reshape(
            n_seq, MLA_HEADS * t, ROPE_DIM)
        o = _mla_decode(q_nope, q_rope_rows, ckv, kr, past["mla"][0][0], past["mla"][1][0].transpose(0, 2, 1),
                        past["page_table"], w["uk_t"], w["uv_t"], t)
    x = _outproj([o], [w["o"]], gains[1, 1], x, ROW_TILE)
    x, conv1 = ffn(x, 1)

    new = (mk.reshape(1, n_seq, t, MOBA_KV_HEADS, HEAD_DIM), mv.reshape(1, n_seq, t, MOBA_KV_HEADS, HEAD_DIM),
           dk.reshape(1, n_seq, t, DIFF_KV_HEADS, 2, DIFF_DIM), dv.reshape(1, n_seq, t, DIFF_KV_HEADS, 2 * DIFF_DIM),
           ckv.reshape(1, n_seq, t, KV_LORA), kr.reshape(1, n_seq, t, ROPE_DIM), jnp.stack([conv0, conv1]))
    return x.reshape(n_seq, t, d), new


def kernel(x_prompt, x_sample, cache_moba_k, cache_moba_v, cache_diff_k, cache_diff_v, cache_mla_ckv,
           cache_mla_krope, state_ffn_conv, page_table, norm_gains, w_in_ab, w_out_ab, diff_lambda, diff_subln,
           mla_w_dq, mla_g_q, mla_w_uq, mla_w_dkv, mla_g_kv, mla_w_uk, mla_w_uv, mla_w_o,
           ffn_w_gate, ffn_w_up, ffn_conv_w, ffn_conv_b, ffn_w_down):
    w = _prep_weights(w_in_ab, w_out_ab, mla_w_dq, mla_w_uq, mla_w_dkv, mla_w_uk, mla_w_uv, mla_w_o,
                      ffn_w_gate, ffn_w_up, ffn_w_down)
    prm = dict(norm_gains=norm_gains, diff_lambda=diff_lambda, diff_subln=diff_subln, mla_g_q=mla_g_q,
               mla_g_kv=mla_g_kv, ffn_conv_w=ffn_conv_w, ffn_conv_b=ffn_conv_b)
    past = dict(moba=(cache_moba_k, cache_moba_v), diff=(cache_diff_k, cache_diff_v),
                mla=(cache_mla_ckv, cache_mla_krope), conv=state_ffn_conv, page_table=page_table)
    y_prompt, new_p = _trunk(x_prompt, 0, w, prm, None)
    y_sample, new_s = _trunk(x_sample, PAST_LEN, w, prm, past)
    return (y_prompt, y_sample) + new_p + new_s
```

```python
import functools
import math

import jax
import jax.numpy as jnp
from jax import lax
from jax.experimental import pallas as pl
from jax.experimental.pallas import tpu as pltpu

D_MODEL = 2048
PAST_LEN = 8192
PAGE_SIZE = 128
HEAD_DIM = 128
MOBA_HEADS = 8
MOBA_KV_HEADS = 2
MOBA_BLOCK = 256
MOBA_TOPK = 3
DIFF_HEADS = 8
DIFF_KV_HEADS = 2
DIFF_DIM = 64
MLA_HEADS = 16
Q_LORA = 512
KV_LORA = 512
NOPE_DIM = 128
ROPE_DIM = 64
V_DIM = 128
D_FF = 5632
CONV_W = 3
ROPE_THETA = 10000.0
EPS = 1e-6

LANES = 128
SUBLANES = 8
VMEM_LIMIT = 56 * 1024 * 1024
NEG_INF = float("-inf")
BF16 = jnp.bfloat16
F32 = jnp.float32


def _cparams(sem):
    return pltpu.CompilerParams(dimension_semantics=sem, vmem_limit_bytes=VMEM_LIMIT)


def _rms(x, g):
    return x * lax.rsqrt(jnp.mean(x * x, axis=-1, keepdims=True) + EPS) * g


def _dot(a, b):
    return jnp.dot(a, b, preferred_element_type=F32)


def _dot_nt(a, b, precision=None):
    return lax.dot_general(a, b, (((1,), (1,)), ((), ())), preferred_element_type=F32, precision=precision)


def _rope_tile(y, cos, sin, width):
    if width == LANES:
        partner = pltpu.roll(y, LANES // 2, 1)
    else:
        lane = lax.broadcasted_iota(jnp.int32, y.shape, 1)
        half = width // 2
        partner = jnp.where(lane % width < half, pltpu.roll(y, LANES - half, 1), pltpu.roll(y, half, 1))
    return y * cos + partner * sin


def _proj_kernel(*refs, n_w, norm_in, epis, col_chunk):
    it = iter(refs)
    x_ref = next(it)
    g_ref = next(it) if norm_in else None
    w_refs = [next(it) for _ in range(n_w)]
    e_refs = []
    for e in epis:
        if e in ("rope128", "rope64"):
            e_refs.append((next(it), next(it)))
        elif e == "rms":
            e_refs.append((next(it),))
        else:
            e_refs.append(())
    o_refs = [next(it) for _ in range(n_w)]

    x = x_ref[...]
    if norm_in:
        x = _rms(x, g_ref[...])
    xb = x.astype(BF16)
    for w_ref, e, er, o_ref in zip(w_refs, epis, e_refs, o_refs):
        n = w_ref.shape[1]
        if e == "rms":
            o_ref[...] = _rms(_dot(xb, w_ref[...]), er[0][...])
            continue
        for c0 in range(0, n, col_chunk):
            c1 = min(n, c0 + col_chunk)
            y = _dot(xb, w_ref[:, c0:c1])
            if e == "none":
                o_ref[:, c0:c1] = y
            else:
                width = LANES if e == "rope128" else DIFF_DIM
                cos, sin = er[0][...], er[1][...]
                for t0 in range(0, c1 - c0, LANES):
                    o_ref[:, c0 + t0:c0 + t0 + LANES] = _rope_tile(y[:, t0:t0 + LANES], cos, sin, width)


def _proj(x, gain, weights, epis, extras, out_widths, tm):
    m, k = x.shape
    assert m % tm == 0
    row = lambda i: (i, 0)
    fixed = lambda i: (0, 0)
    args, specs = [x], [pl.BlockSpec((tm, k), row)]
    if gain is not None:
        args.append(gain.reshape(1, k))
        specs.append(pl.BlockSpec((1, k), fixed))
    for w in weights:
        args.append(w)
        specs.append(pl.BlockSpec(w.shape, fixed, pipeline_mode=pl.Buffered(1)))
    for e, ex in zip(epis, extras):
        if e in ("rope128", "rope64"):
            for t in ex:
                args.append(t)
                specs.append(pl.BlockSpec((tm, LANES), row))
        elif e == "rms":
            args.append(ex[0].reshape(1, -1))
            specs.append(pl.BlockSpec((1, ex[0].size), fixed))
    out_shape = [jax.ShapeDtypeStruct((m, n), F32) for n in out_widths]
    out_specs = [pl.BlockSpec((tm, n), row) for n in out_widths]
    kern = functools.partial(_proj_kernel, n_w=len(weights), norm_in=gain is not None, epis=tuple(epis),
                             col_chunk=512)
    return pl.pallas_call(kern, grid=(m // tm,), in_specs=specs, out_specs=out_specs, out_shape=out_shape,
                          compiler_params=_cparams(("parallel",)))(*args)


def _outproj_kernel(*refs, n_a):
    a_refs = refs[:n_a]
    w_refs = refs[n_a:2 * n_a]
    g_ref, x_ref, o_ref = refs[2 * n_a:]
    y = _dot(a_refs[0][...].astype(BF16), w_refs[0][...])
    for a_ref, w_ref in zip(a_refs[1:], w_refs[1:]):
        y = y + _dot(a_ref[...].astype(BF16), w_ref[...])
    o_ref[...] = x_ref[...] + _rms(y, g_ref[...])


def _outproj(acts, weights, gain, resid, tm):
    m, d = resid.shape
    row = lambda i: (i, 0)
    fixed = lambda i: (0, 0)
    specs = [pl.BlockSpec((tm, a.shape[1]), row) for a in acts]
    specs += [pl.BlockSpec(w.shape, fixed, pipeline_mode=pl.Buffered(1)) for w in weights]
    specs += [pl.BlockSpec((1, d), fixed), pl.BlockSpec((tm, d), row)]
    return pl.pallas_call(functools.partial(_outproj_kernel, n_a=len(acts)), grid=(m // tm,), in_specs=specs,
                          out_specs=pl.BlockSpec((tm, d), row), out_shape=jax.ShapeDtypeStruct((m, d), F32),
                          compiler_params=_cparams(("parallel",)))(*acts, *weights, gain.reshape(1, d), resid)


def _ffn_kernel(x_ref, g_in_ref, wg_ref, wu_ref, cw_ref, cb_ref, wd_ref, g_out_ref, past_ref,
                y_ref, tail_ref, h_s, acc_s, gbuf_s, carry_s, *, blocks_per_seq):
    i, f = pl.program_id(0), pl.program_id(1)
    n_seq, t_blk, tf = gbuf_s.shape[0], gbuf_s.shape[1] - SUBLANES, gbuf_s.shape[2]
    halo = CONV_W - 1

    @pl.when(f == 0)
    def _():
        h_s[...] = _rms(x_ref[...], g_in_ref[...]).astype(BF16)
        acc_s[...] = jnp.zeros_like(acc_s)
        if blocks_per_seq > 1:
            @pl.when(i == 0)
            def _():
                carry_s[...] = jnp.zeros_like(carry_s)

    h = h_s[...]
    g3 = _dot(h, wg_ref[...]).reshape(n_seq, t_blk, tf)
    u = _dot(h, wu_ref[...])

    if blocks_per_seq == 1:
        gbuf_s[:, 0:SUBLANES, :] = past_ref[...]
    else:
        gbuf_s[:, 0:SUBLANES, :] = jnp.where(i % blocks_per_seq == 0, past_ref[...], carry_s[f])
        carry_s[f] = g3[:, t_blk - SUBLANES:, :]
    gbuf_s[:, SUBLANES:, :] = g3
    tail_ref[...] = g3[:, t_blk - halo:, :]

    gc = cb_ref[...].reshape(1, 1, tf)
    for j in range(CONV_W):
        gc = gc + cw_ref[j:j + 1, :].reshape(1, 1, tf) * gbuf_s[:, SUBLANES - halo + j:SUBLANES - halo + j + t_blk, :]
    act = jax.nn.gelu(gc.reshape(n_seq * t_blk, tf), approximate=True) * u
    acc_s[...] += _dot(act.astype(BF16), wd_ref[...])

    @pl.when(f == pl.num_programs(1) - 1)
    def _():
        y_ref[...] = x_ref[...] + _rms(acc_s[...], g_out_ref[...])


def _ffn(x, past8, g_in, wg, wu, cw, cb, wd, g_out, seq_len, tm, tf):
    m, d = x.shape
    n_seq = m // seq_len
    if seq_len >= tm:
        blocks_per_seq, seq_per_blk, t_blk = seq_len // tm, 1, tm
    else:
        blocks_per_seq, seq_per_blk, t_blk = 1, tm // seq_len, seq_len
    n_f = D_FF // tf
    halo = CONV_W - 1
    seq_idx = lambda i, f: (i // blocks_per_seq, 0, f)
    in_specs = [
        pl.BlockSpec((tm, d), lambda i, f: (i, 0)),
        pl.BlockSpec((1, d), lambda i, f: (0, 0)),
        pl.BlockSpec((d, tf), lambda i, f: (0, f)),
        pl.BlockSpec((d, tf), lambda i, f: (0, f)),
        pl.BlockSpec((CONV_W, tf), lambda i, f: (0, f)),
        pl.BlockSpec((1, tf), lambda i, f: (0, f)),
        pl.BlockSpec((tf, d), lambda i, f: (f, 0)),
        pl.BlockSpec((1, d), lambda i, f: (0, 0)),
        pl.BlockSpec((seq_per_blk, SUBLANES, tf), seq_idx),
    ]
    out_specs = [pl.BlockSpec((tm, d), lambda i, f: (i, 0)),
                 pl.BlockSpec((seq_per_blk, halo, tf), lambda i, f: (i, 0, f))]
    out_shape = [jax.ShapeDtypeStruct((m, d), F32),
                 jax.ShapeDtypeStruct((m // tm * seq_per_blk, halo, D_FF), F32)]
    scratch = [pltpu.VMEM((tm, d), BF16), pltpu.VMEM((tm, d), F32),
               pltpu.VMEM((seq_per_blk, t_blk + SUBLANES, tf), F32),
               pltpu.VMEM((n_f, seq_per_blk, SUBLANES, tf), F32)]
    kern = functools.partial(_ffn_kernel, blocks_per_seq=blocks_per_seq)
    y, tails = pl.pallas_call(kern, grid=(m // tm, n_f), in_specs=in_specs, out_specs=out_specs,
                              out_shape=out_shape, scratch_shapes=scratch,
                              compiler_params=_cparams(("arbitrary", "arbitrary")))(
        x, g_in.reshape(1, d), wg, wu, cw, cb.reshape(1, D_FF), wd, g_out.reshape(1, d), past8)
    return y, tails[blocks_per_seq - 1::blocks_per_seq]


def _weighted_values(p, values):
    if not isinstance(values, (list, tuple)):
        return _dot(p, values)
    r = p.shape[0] // len(values)
    return jnp.concatenate([_dot(p[i * r:(i + 1) * r], v) for i, v in enumerate(values)], axis=0)


def _softmax_init(s, values, m_s, l_s, acc_s):
    m = jnp.max(s, axis=-1, keepdims=True)
    p = jnp.exp(s - m)
    m_s[...] = m
    l_s[...] = jnp.sum(p, axis=-1, keepdims=True)
    acc_s[...] = _weighted_values(p.astype(BF16), values)


def _softmax_update(s, values, m_s, l_s, acc_s):
    m_old = m_s[...]
    m_new = jnp.maximum(m_old, jnp.max(s, axis=-1, keepdims=True))
    alpha = jnp.exp(m_old - m_new)
    p = jnp.exp(s - m_new)
    m_s[...] = m_new
    l_s[...] = alpha * l_s[...] + jnp.sum(p, axis=-1, keepdims=True)
    acc_s[...] = alpha * acc_s[...] + _weighted_values(p.astype(BF16), values)


def _topk_mask(gate, valid, k, axis=1):
    lane = lax.broadcasted_iota(jnp.int32, gate.shape, axis)
    big = jnp.int32(gate.shape[axis])
    sel = jnp.zeros(gate.shape, jnp.bool_)
    rem = valid
    for _ in range(k):
        gm = jnp.where(rem, gate, NEG_INF)
        top = jnp.max(gm, axis=axis, keepdims=True)
        idx = jnp.min(jnp.where(rem & (gm == top), lane, big), axis=axis, keepdims=True)
        pick = lane == idx
        sel = sel | pick
        rem = rem & jnp.logical_not(pick)
    return sel


def _causal_tile(rows, cols, period):
    r = lax.broadcasted_iota(jnp.int32, (rows, cols), 0) % period
    c = lax.broadcasted_iota(jnp.int32, (rows, cols), 1)
    return c <= r


def _diff_lambda(lam_ref, lam_init):
    lv = lam_ref[...]
    d1 = jnp.sum(lv[0:1] * lv[1:2], axis=-1, keepdims=True)
    d2 = jnp.sum(lv[2:3] * lv[3:4], axis=-1, keepdims=True)
    return jnp.exp(d1) - jnp.exp(d2) + lam_init


def _split_maps(q2):
    lane = lax.broadcasted_iota(jnp.int32, q2.shape, 1)
    lo = jnp.where(lane < DIFF_DIM, q2, 0.0)
    hi = jnp.where(lane >= DIFF_DIM, q2, 0.0)
    return jnp.concatenate([lo, hi], axis=0)


TQ = MOBA_BLOCK
DIFF_HEAD_GROUP = 2
MLA_HEAD_GROUP = 4
LOG2_E = math.log2(math.e)


def _lane_halves_max(s):
    out = s[:, :LANES]
    for c0 in range(LANES, s.shape[1], LANES):
        out = jnp.maximum(out, s[:, c0:c0 + LANES])
    return out


def _lane_halves_sum(p):
    out = p[:, :LANES]
    for c0 in range(LANES, p.shape[1], LANES):
        out = out + p[:, c0:c0 + LANES]
    return out


def _two_pass_attend(qk, values, own_mask, past_bias, qi, scale, mx_s, l_s, acc_s):
    def raw_scores(j, k0):
        s = qk(k0)
        if past_bias is not None:
            s = s + jnp.concatenate([past_bias(j)] * (TQ // LANES), axis=1)
        return s

    own0 = pl.multiple_of(qi * TQ, TQ)
    s_own = jnp.where(own_mask, qk(own0), NEG_INF)
    mx_s[...] = _lane_halves_max(s_own)

    def pass1(j, carry):
        mx_s[...] = jnp.maximum(mx_s[...], _lane_halves_max(raw_scores(j, pl.multiple_of(j * TQ, TQ))))
        return carry

    lax.fori_loop(0, qi, pass1, 0)
    mx_s[...] = jnp.broadcast_to(jnp.max(mx_s[...], axis=-1, keepdims=True), mx_s.shape)

    def probs(s):
        m = jnp.concatenate([mx_s[...]] * (TQ // LANES), axis=1)
        return jnp.exp2((s - m) * (scale * LOG2_E))

    p = probs(s_own)
    l_s[...] = _lane_halves_sum(p)
    acc_s[...] = _weighted_values(p.astype(BF16), values(own0))

    def pass2(j, carry):
        k0 = pl.multiple_of(j * TQ, TQ)
        p = probs(raw_scores(j, k0))
        l_s[...] += _lane_halves_sum(p)
        acc_s[...] += _weighted_values(p.astype(BF16), values(k0))
        return carry

    lax.fori_loop(0, qi, pass2, 0)
    return acc_s[...] / jnp.sum(l_s[...], axis=-1, keepdims=True)


def _moba_prefill_kernel(q_ref, k_ref, v_ref, o_ref, kb_s, vb_s, bias_s, mx_s, l_s, acc_s):
    qi = pl.program_id(2)
    t = k_ref.shape[0]
    nb = t // MOBA_BLOCK
    n_sel = max(1, min(MOBA_TOPK, nb - 1))
    scale = HEAD_DIM ** -0.5
    g_per = MOBA_HEADS // MOBA_KV_HEADS

    @pl.when(qi == 0)
    def _():
        kb_s[...] = k_ref[...].astype(BF16)
        vb_s[...] = v_ref[...].astype(BF16)

    kmean = jnp.mean(k_ref[...].reshape(nb, MOBA_BLOCK, HEAD_DIM), axis=1)
    rows = g_per * TQ
    qf = _rows_by_head(q_ref, range(g_per), HEAD_DIM)
    qb = qf.astype(BF16)
    blk = lax.broadcasted_iota(jnp.int32, (nb, rows), 0)
    gate = _dot_nt(kmean, qf, lax.Precision.HIGHEST)
    sel = _topk_mask(gate, blk < qi, n_sel, axis=0).astype(F32)
    spread = (lax.broadcasted_iota(jnp.int32, (nb, nb * LANES), 1) // LANES
              == lax.broadcasted_iota(jnp.int32, (nb, nb * LANES), 0)).astype(F32)
    keep = lax.dot_general(sel, spread, (((0,), (0,)), ((), ())), preferred_element_type=F32)
    for j in range(nb):
        bias_s[j] = jnp.where(keep[:, j * LANES:(j + 1) * LANES] > 0.5, 0.0, NEG_INF)
    o = _two_pass_attend(lambda k0: _dot_nt(qb, kb_s[pl.ds(k0, TQ), :]), lambda k0: vb_s[pl.ds(k0, TQ), :],
                         _causal_tile(rows, TQ, TQ), lambda j: bias_s[j], qi, scale, mx_s, l_s, acc_s)
    for g in range(g_per):
        o_ref[:, g * HEAD_DIM:(g + 1) * HEAD_DIM] = o[g * TQ:(g + 1) * TQ]


def _moba_prefill(mq, mk, mv, n_batch, t):
    m = mq.shape[0]
    nq = t // TQ
    g_w = MOBA_HEADS // MOBA_KV_HEADS * HEAD_DIM
    q_spec = pl.BlockSpec((TQ, g_w), lambda b, k, i: (b * nq + i, k))
    kv_spec = pl.BlockSpec((t, HEAD_DIM), lambda b, k, i: (b, k))
    rows = MOBA_HEADS // MOBA_KV_HEADS * TQ
    scratch = [pltpu.VMEM((t, HEAD_DIM), BF16), pltpu.VMEM((t, HEAD_DIM), BF16),
               pltpu.VMEM((t // MOBA_BLOCK, rows, LANES), F32),
               pltpu.VMEM((rows, LANES), F32), pltpu.VMEM((rows, LANES), F32), pltpu.VMEM((rows, HEAD_DIM), F32)]
    return pl.pallas_call(_moba_prefill_kernel, grid=(n_batch, MOBA_KV_HEADS, nq),
                          in_specs=[q_spec, kv_spec, kv_spec], out_specs=q_spec,
                          out_shape=jax.ShapeDtypeStruct((m, MOBA_HEADS * HEAD_DIM), F32), scratch_shapes=scratch,
                          compiler_params=_cparams(("parallel", "parallel", "arbitrary")))(mq, mk, mv)


def _diff_finish(o2, lam, sub_g, lam_init, rows):
    o = o2[:rows] - lam * o2[rows:]
    return _rms(o, sub_g) * (1.0 - lam_init)


def _diff_prefill_kernel(q_ref, k_ref, v_ref, lam_ref, subg_ref, o_ref, kb_s, vb_s, mx_s, l_s, acc_s, *, lam_init):
    qi = pl.program_id(2)
    scale = DIFF_DIM ** -0.5
    g_per = DIFF_HEADS // DIFF_KV_HEADS
    width = 2 * DIFF_DIM

    @pl.when(qi == 0)
    def _():
        kb_s[...] = k_ref[...].astype(BF16)
        vb_s[...] = v_ref[...].astype(BF16)

    lam = _diff_lambda(lam_ref, lam_init)
    grp = mx_s.shape[0] // (2 * TQ)
    tril = _causal_tile(grp * 2 * TQ, TQ, TQ)
    for g0 in range(0, g_per, grp):
        qb = jnp.concatenate([_split_maps(q_ref[:, g * width:(g + 1) * width]) for g in range(g0, g0 + grp)],
                             axis=0).astype(BF16)
        o2 = _two_pass_attend(lambda k0: _dot_nt(qb, kb_s[pl.ds(k0, TQ), :]), lambda k0: vb_s[pl.ds(k0, TQ), :],
                              tril, None, qi, scale, mx_s, l_s, acc_s)
        for i in range(grp):
            g = g0 + i
            o_ref[:, g * width:(g + 1) * width] = _diff_finish(
                o2[i * 2 * TQ:(i + 1) * 2 * TQ], lam, subg_ref[...], lam_init, TQ)


def _diff_prefill(dq, dk, dv, lam_vec, sub_g, lam_init, n_batch, t):
    m = dq.shape[0]
    nq = t // TQ
    width = 2 * DIFF_DIM
    g_w = DIFF_HEADS // DIFF_KV_HEADS * width
    q_spec = pl.BlockSpec((TQ, g_w), lambda b, k, i: (b * nq + i, k))
    kv_spec = pl.BlockSpec((t, width), lambda b, k, i: (b, k))
    fixed = lambda b, k, i: (0, 0)
    rows = DIFF_HEAD_GROUP * 2 * TQ
    scratch = [pltpu.VMEM((t, width), BF16), pltpu.VMEM((t, width), BF16),
               pltpu.VMEM((rows, LANES), F32), pltpu.VMEM((rows, LANES), F32), pltpu.VMEM((rows, width), F32)]
    kern = functools.partial(_diff_prefill_kernel, lam_init=lam_init)
    return pl.pallas_call(kern, grid=(n_batch, DIFF_KV_HEADS, nq),
                          in_specs=[q_spec, kv_spec, kv_spec, pl.BlockSpec((4, DIFF_DIM), fixed),
                                    pl.BlockSpec((1, width), fixed)],
                          out_specs=q_spec, out_shape=jax.ShapeDtypeStruct((m, DIFF_HEADS * width), F32),
                          scratch_shapes=scratch,
                          compiler_params=_cparams(("parallel", "parallel", "arbitrary")))(
        dq, dk, dv, lam_vec, sub_g.reshape(1, width))


def _mla_prefill_kernel(qn_ref, qr_ref, ckv_ref, kr2_ref, wuk_ref, wuv_ref, o_ref, kx_s, vx_s, mx_s, l_s, acc_s):
    qi = pl.program_id(2)
    grp = kx_s.shape[0]
    scale = (NOPE_DIM + ROPE_DIM) ** -0.5

    @pl.when(qi == 0)
    def _():
        cb = ckv_ref[...].astype(BF16)
        kr = kr2_ref[...].astype(BF16)
        for i in range(grp):
            kx_s[i, :, :NOPE_DIM] = _dot_nt(cb, wuk_ref[i]).astype(BF16)
            kx_s[i, :, NOPE_DIM:] = kr
            vx_s[i] = _dot(cb, wuv_ref[i]).astype(BF16)

    lane = lax.broadcasted_iota(jnp.int32, (TQ, LANES), 1)
    q_heads = []
    for i in range(grp):
        pair = qr_ref[:, (i // 2) * LANES:(i // 2 + 1) * LANES]
        in_head = (lane < ROPE_DIM) if i % 2 == 0 else (lane >= ROPE_DIM)
        q_r = jnp.where(in_head, pair, 0.0)
        q_heads.append(jnp.concatenate([qn_ref[:, i * NOPE_DIM:(i + 1) * NOPE_DIM], q_r], axis=1).astype(BF16))

    def qk(k0):
        return jnp.concatenate([_dot_nt(q_heads[i], kx_s[i, pl.ds(k0, TQ), :]) for i in range(grp)], axis=0)

    o = _two_pass_attend(qk, lambda k0: [vx_s[i, pl.ds(k0, TQ), :] for i in range(grp)],
                         _causal_tile(grp * TQ, TQ, TQ), None, qi, scale, mx_s, l_s, acc_s)
    for i in range(grp):
        o_ref[:, i * V_DIM:(i + 1) * V_DIM] = o[i * TQ:(i + 1) * TQ]


def _mla_prefill(q_nope, q_rope, ckv, kr2, wuk_t, wuv_t, n_batch, t):
    m = q_nope.shape[0]
    nq = t // TQ
    grp = MLA_HEAD_GROUP
    row = lambda b, g, i: (b * nq + i, g)
    seq = lambda b, g, i: (b, 0)
    head = lambda b, g, i: (g, 0, 0)
    scratch = [pltpu.VMEM((grp, t, NOPE_DIM + LANES), BF16), pltpu.VMEM((grp, t, V_DIM), BF16),
               pltpu.VMEM((grp * TQ, LANES), F32), pltpu.VMEM((grp * TQ, LANES), F32),
               pltpu.VMEM((grp * TQ, V_DIM), F32)]
    in_specs = [pl.BlockSpec((TQ, grp * NOPE_DIM), row), pl.BlockSpec((TQ, grp * ROPE_DIM), row),
                pl.BlockSpec((t, KV_LORA), seq), pl.BlockSpec((t, LANES), seq),
                pl.BlockSpec((grp,) + wuk_t.shape[1:], head), pl.BlockSpec((grp,) + wuv_t.shape[1:], head)]
    return pl.pallas_call(_mla_prefill_kernel, grid=(n_batch, MLA_HEADS // grp, nq), in_specs=in_specs,
                          out_specs=pl.BlockSpec((TQ, grp * V_DIM), row),
                          out_shape=jax.ShapeDtypeStruct((m, MLA_HEADS * V_DIM), F32), scratch_shapes=scratch,
                          compiler_params=_cparams(("parallel", "parallel", "arbitrary")))(
        q_nope, q_rope, ckv, kr2, wuk_t, wuv_t)


PAGES_PER_STEP = 16
N_PAGES = PAST_LEN // PAGE_SIZE


def _page_copy(pt_ref, pool, buf, sem, step, slot, u, pages_per_step):
    return pltpu.make_async_copy(pool.at[pt_ref[step * pages_per_step + u]], buf.at[slot, u], sem.at[slot])


def _fetch_pages(pt_ref, pools, bufs, sems, pages_per_step):
    n_chunk = pl.num_programs(1)
    g = pl.program_id(0) * n_chunk + pl.program_id(1)
    slot = g % 2

    def copies(step, slt):
        return [_page_copy(pt_ref, pool, buf, sems.at[i], step, slt, u, pages_per_step)
                for u in range(pages_per_step) for i, (pool, buf) in enumerate(zip(pools, bufs))]

    @pl.when(g == 0)
    def _():
        for cp in copies(0, 0):
            cp.start()

    @pl.when(g + 1 < pl.num_programs(0) * n_chunk)
    def _():
        for cp in copies(g + 1, 1 - slot):
            cp.start()

    for cp in copies(g, slot):
        cp.wait()
    return [[buf.at[slot, u] for u in range(pages_per_step)] for buf in bufs]


def _paged_operands(pools, pages_per_step):
    specs = [pl.BlockSpec(memory_space=pl.ANY) for _ in pools]
    bufs = [pltpu.VMEM((2, pages_per_step) + pool.shape[1:], pool.dtype) for pool in pools]
    return specs, bufs, pltpu.SemaphoreType.DMA((len(pools), 2))


def _head_rows(refs, kvh, n_kvh):
    return jnp.concatenate([r[pl.ds(kvh, PAGE_SIZE, stride=n_kvh), :] for r in refs], axis=0)


def _rows_by_head(ref, heads, width):
    return jnp.concatenate([ref[:, h * width:(h + 1) * width] for h in heads], axis=0)


def _moba_decode_kernel(pt_ref, q_ref, kn_ref, vn_ref, k_pool, v_pool, o_ref, m_s, l_s, o_s, km_s,
                        k_buf, v_buf, sems, *, pages_per_step, t_new):
    p = pages_per_step
    k_refs, v_refs = _fetch_pages(pt_ref, (k_pool, v_pool), (k_buf, v_buf), sems, p)
    c = pl.program_id(1)
    scale = HEAD_DIM ** -0.5
    g_per = MOBA_HEADS // MOBA_KV_HEADS
    rows = g_per * t_new
    pages_per_blk = MOBA_BLOCK // PAGE_SIZE
    blk_per_step = p // pages_per_blk
    n_past_blk = PAST_LEN // MOBA_BLOCK
    n_sel = max(1, min(MOBA_TOPK, n_past_blk))

    lane = lax.broadcasted_iota(jnp.int32, (rows, LANES), 1)

    @pl.when(c == 0)
    def _():
        m_s[...] = jnp.zeros_like(m_s)
        l_s[...] = jnp.zeros_like(l_s)

    for kvh in range(MOBA_KV_HEADS):
        q_rows = _rows_by_head(q_ref, range(kvh * g_per, (kvh + 1) * g_per), HEAD_DIM)
        qb = q_rows.astype(BF16)
        kf = _head_rows(k_refs, kvh, MOBA_KV_HEADS)
        vb = _head_rows(v_refs, kvh, MOBA_KV_HEADS).astype(BF16)
        s_all = _dot_nt(qb, kf.astype(BF16)) * scale
        m_mat, l_mat = m_s[kvh], l_s[kvh]
        for b in range(blk_per_step):
            k0, k1 = b * MOBA_BLOCK, (b + 1) * MOBA_BLOCK
            s = s_all[:, k0:k1]
            m = jnp.max(s, axis=-1, keepdims=True)
            e = jnp.exp(s - m)
            blk = c * blk_per_step + b
            m_mat = jnp.where(lane == blk, m, m_mat)
            l_mat = jnp.where(lane == blk, jnp.sum(e, axis=-1, keepdims=True), l_mat)
            o_s[kvh, blk] = _dot(e.astype(BF16), vb[k0:k1])
            km_s[kvh, pl.ds(blk, 1), :] = jnp.mean(kf[k0:k1], axis=0, keepdims=True)
        m_s[kvh] = m_mat
        l_s[kvh] = l_mat

    @pl.when(c == pl.num_programs(1) - 1)
    def _():
        causal = _causal_tile(MOBA_KV_HEADS * rows, t_new, t_new)
        q_k = [_rows_by_head(q_ref, range(k * g_per, (k + 1) * g_per), HEAD_DIM) for k in range(MOBA_KV_HEADS)]
        gate = jnp.concatenate([_dot_nt(q_k[k], km_s[k], lax.Precision.HIGHEST) for k in range(MOBA_KV_HEADS)],
                               axis=0)
        sel = _topk_mask(gate, jnp.ones(gate.shape, jnp.bool_), n_sel)
        s_own = jnp.concatenate(
            [_dot_nt(q_k[k].astype(BF16), kn_ref[:, k * HEAD_DIM:(k + 1) * HEAD_DIM].astype(BF16))
             for k in range(MOBA_KV_HEADS)], axis=0) * scale
        s_own = jnp.where(causal, s_own, NEG_INF)
        m_blk = jnp.concatenate([m_s[k] for k in range(MOBA_KV_HEADS)], axis=0)[:, :n_past_blk]
        l_blk = jnp.concatenate([l_s[k] for k in range(MOBA_KV_HEADS)], axis=0)[:, :n_past_blk]
        m_fin = jnp.maximum(jnp.max(s_own, axis=-1, keepdims=True),
                            jnp.max(jnp.where(sel, m_blk, NEG_INF), axis=-1, keepdims=True))
        e_own = jnp.exp(s_own - m_fin)
        w = jnp.exp(jnp.where(sel, m_blk - m_fin, NEG_INF))
        l_fin = jnp.sum(e_own, axis=-1, keepdims=True) + jnp.sum(w * l_blk, axis=-1, keepdims=True)
        for k in range(MOBA_KV_HEADS):
            r0, r1 = k * rows, (k + 1) * rows
            parts = [_dot(e_own[r0:r1].astype(BF16), vn_ref[:, k * HEAD_DIM:(k + 1) * HEAD_DIM].astype(BF16)),
                     jnp.zeros((rows, HEAD_DIM), F32)]
            for b in range(n_past_blk):
                parts[b % 2] = parts[b % 2] + w[r0:r1, b:b + 1] * o_s[k, b]
            o = (parts[0] + parts[1]) / l_fin[r0:r1]
            for i in range(g_per):
                h = k * g_per + i
                o_ref[:, h * HEAD_DIM:(h + 1) * HEAD_DIM] = o[i * t_new:(i + 1) * t_new]


def _moba_decode(mq, mk, mv, pool_k, pool_v, page_table, t_new):
    n_seq = mq.shape[0] // t_new
    p = PAGES_PER_STEP
    kv_w = MOBA_KV_HEADS * HEAD_DIM
    g_per = MOBA_HEADS // MOBA_KV_HEADS
    rows = g_per * t_new
    n_blk = PAST_LEN // MOBA_BLOCK
    seq = lambda s, c, pt: (s, 0)
    pool_specs, bufs, sems = _paged_operands((pool_k, pool_v), p)
    in_specs = [pl.BlockSpec((t_new, MOBA_HEADS * HEAD_DIM), seq), pl.BlockSpec((t_new, kv_w), seq),
                pl.BlockSpec((t_new, kv_w), seq)] + pool_specs
    assert n_blk <= LANES
    scratch = [pltpu.VMEM((MOBA_KV_HEADS, rows, LANES), F32), pltpu.VMEM((MOBA_KV_HEADS, rows, LANES), F32),
               pltpu.VMEM((MOBA_KV_HEADS, n_blk, rows, HEAD_DIM), F32), pltpu.VMEM((MOBA_KV_HEADS, n_blk, HEAD_DIM), F32)]
    grid_spec = pltpu.PrefetchScalarGridSpec(
        num_scalar_prefetch=1, grid=(n_seq, N_PAGES // p), in_specs=in_specs,
        out_specs=pl.BlockSpec((t_new, MOBA_HEADS * HEAD_DIM), seq), scratch_shapes=scratch + bufs + [sems])
    kern = functools.partial(_moba_decode_kernel, pages_per_step=p, t_new=t_new)
    return pl.pallas_call(kern, grid_spec=grid_spec, out_shape=jax.ShapeDtypeStruct(mq.shape, F32),
                          compiler_params=_cparams(("arbitrary", "arbitrary")))(
        page_table.reshape(-1), mq, mk, mv, pool_k, pool_v)


def _diff_decode_kernel(pt_ref, q_ref, kn_ref, vn_ref, lam_ref, subg_ref, k_pool, v_pool, o_ref, m_s, l_s, acc_s,
                        k_buf, v_buf, sems, *, pages_per_step, t_new, lam_init):
    p = pages_per_step
    k_refs, v_refs = _fetch_pages(pt_ref, (k_pool, v_pool), (k_buf, v_buf), sems, p)
    c = pl.program_id(1)
    scale = DIFF_DIM ** -0.5
    g_per = DIFF_HEADS // DIFF_KV_HEADS
    width = 2 * DIFF_DIM
    rows = g_per * t_new
    kvhs = range(DIFF_KV_HEADS)
    qb = [_split_maps(_rows_by_head(q_ref, range(k * g_per, (k + 1) * g_per), width)).astype(BF16) for k in kvhs]

    @pl.when(c == 0)
    def _():
        causal = _causal_tile(DIFF_KV_HEADS * 2 * rows, t_new, t_new)
        s_own = jnp.concatenate([_dot_nt(qb[k], kn_ref[:, k * width:(k + 1) * width].astype(BF16)) for k in kvhs],
                                axis=0) * scale
        _softmax_init(jnp.where(causal, s_own, NEG_INF),
                      [vn_ref[:, k * width:(k + 1) * width].astype(BF16) for k in kvhs], m_s, l_s, acc_s)

    s = jnp.concatenate(
        [_dot(qb[k], jnp.concatenate([r[k * width:(k + 1) * width, :] for r in k_refs], axis=1).astype(BF16))
         for k in kvhs], axis=0) * scale
    _softmax_update(s, [_head_rows(v_refs, k, DIFF_KV_HEADS).astype(BF16) for k in kvhs], m_s, l_s, acc_s)

    @pl.when(c == pl.num_programs(1) - 1)
    def _():
        lam = _diff_lambda(lam_ref, lam_init)
        o2 = acc_s[...] / l_s[...]
        for k in kvhs:
            o = _diff_finish(o2[k * 2 * rows:(k + 1) * 2 * rows], lam, subg_ref[...], lam_init, rows)
            for i in range(g_per):
                h = k * g_per + i
                o_ref[:, h * width:(h + 1) * width] = o[i * t_new:(i + 1) * t_new]


def _diff_decode(dq, dk, dv, pool_k, pool_v, page_table, lam_vec, sub_g, lam_init, t_new):
    n_seq = dq.shape[0] // t_new
    p = PAGES_PER_STEP
    width = 2 * DIFF_DIM
    kv_w = DIFF_KV_HEADS * width
    rows2 = 2 * (DIFF_HEADS // DIFF_KV_HEADS) * t_new
    seq = lambda s, c, pt: (s, 0)
    fixed = lambda s, c, pt: (0, 0)
    in_specs = [pl.BlockSpec((t_new, DIFF_HEADS * width), seq), pl.BlockSpec((t_new, kv_w), seq),
                pl.BlockSpec((t_new, kv_w), seq), pl.BlockSpec((4, DIFF_DIM), fixed),
                pl.BlockSpec((1, width), fixed)]
    pool_specs, bufs, sems = _paged_operands((pool_k, pool_v), p)
    scratch = [pltpu.VMEM((DIFF_KV_HEADS * rows2, 1), F32), pltpu.VMEM((DIFF_KV_HEADS * rows2, 1), F32),
               pltpu.VMEM((DIFF_KV_HEADS * rows2, width), F32)]
    grid_spec = pltpu.PrefetchScalarGridSpec(
        num_scalar_prefetch=1, grid=(n_seq, N_PAGES // p), in_specs=in_specs + pool_specs,
        out_specs=pl.BlockSpec((t_new, DIFF_HEADS * width), seq), scratch_shapes=scratch + bufs + [sems])
    kern = functools.partial(_diff_decode_kernel, pages_per_step=p, t_new=t_new, lam_init=lam_init)
    return pl.pallas_call(kern, grid_spec=grid_spec, out_shape=jax.ShapeDtypeStruct(dq.shape, F32),
                          compiler_params=_cparams(("arbitrary", "arbitrary")))(
        page_table.reshape(-1), dq, dk, dv, lam_vec, sub_g.reshape(1, width), pool_k, pool_v)


def _mla_decode_kernel(pt_ref, qn_ref, qr_ref, cn_ref, krn_ref, wuk_ref, wuv_ref, c_pool, r_pool, o_ref,
                       qlat_s, m_s, l_s, acc_s, c_buf, r_buf, sems, *, pages_per_step, t_new):
    p = pages_per_step
    c_refs, r_refs = _fetch_pages(pt_ref, (c_pool, r_pool), (c_buf, r_buf), sems, p)
    c = pl.program_id(1)
    scale = (NOPE_DIM + ROPE_DIM) ** -0.5
    rows = MLA_HEADS * t_new

    @pl.when(c == 0)
    def _():
        for h in range(MLA_HEADS):
            qlat_s[h * t_new:(h + 1) * t_new, :] = _dot(
                qn_ref[:, h * NOPE_DIM:(h + 1) * NOPE_DIM].astype(BF16), wuk_ref[h])
        q_lat = qlat_s[...].astype(BF16)
        causal = _causal_tile(rows, t_new, t_new)
        cn = cn_ref[...].astype(BF16)
        s_own = (_dot_nt(q_lat, cn) + _dot_nt(qr_ref[...].astype(BF16), krn_ref[...].astype(BF16))) * scale
        _softmax_init(jnp.where(causal, s_own, NEG_INF), cn, m_s, l_s, acc_s)

    q_lat = qlat_s[...].astype(BF16)
    ckv = jnp.concatenate([r[...] for r in c_refs], axis=0).astype(BF16)
    kr_t = jnp.concatenate([r[...] for r in r_refs], axis=1).astype(BF16)
    s = (_dot_nt(q_lat, ckv) + _dot(qr_ref[...].astype(BF16), kr_t)) * scale
    _softmax_update(s, ckv, m_s, l_s, acc_s)

    @pl.when(c == pl.num_programs(1) - 1)
    def _():
        o_lat = (acc_s[...] / l_s[...]).astype(BF16)
        for h in range(MLA_HEADS):
            o_ref[:, h * V_DIM:(h + 1) * V_DIM] = _dot(o_lat[h * t_new:(h + 1) * t_new], wuv_ref[h])


def _mla_decode(q_nope, q_rope_rows, ckv_new, kr_new, pool_c, pool_r, page_table, wuk_t, wuv_t, t_new):
    n_seq = q_nope.shape[0] // t_new
    p = PAGES_PER_STEP
    rows = MLA_HEADS * t_new
    seq = lambda s, c, pt: (s, 0)
    fixed3 = lambda s, c, pt: (0, 0, 0)
    in_specs = [pl.BlockSpec((t_new, MLA_HEADS * NOPE_DIM), seq),
                pl.BlockSpec((None, rows, ROPE_DIM), lambda s, c, pt: (s, 0, 0)),
                pl.BlockSpec((t_new, KV_LORA), seq), pl.BlockSpec((t_new, ROPE_DIM), seq),
                pl.BlockSpec(wuk_t.shape, fixed3, pipeline_mode=pl.Buffered(1)),
                pl.BlockSpec(wuv_t.shape, fixed3, pipeline_mode=pl.Buffered(1))]
    pool_specs, bufs, sems = _paged_operands((pool_c, pool_r), p)
    scratch = [pltpu.VMEM((rows, KV_LORA), F32), pltpu.VMEM((rows, 1), F32), pltpu.VMEM((rows, 1), F32),
               pltpu.VMEM((rows, KV_LORA), F32)]
    grid_spec = pltpu.PrefetchScalarGridSpec(
        num_scalar_prefetch=1, grid=(n_seq, N_PAGES // p), in_specs=in_specs + pool_specs,
        out_specs=pl.BlockSpec((t_new, MLA_HEADS * V_DIM), seq), scratch_shapes=scratch + bufs + [sems])
    kern = functools.partial(_mla_decode_kernel, pages_per_step=p, t_new=t_new)
    return pl.pallas_call(kern, grid_spec=grid_spec,
                          out_shape=jax.ShapeDtypeStruct((n_seq * t_new, MLA_HEADS * V_DIM), F32),
                          compiler_params=_cparams(("arbitrary", "arbitrary")))(
        page_table.reshape(-1), q_nope, q_rope_rows, ckv_new, kr_new, wuk_t, wuv_t, pool_c, pool_r)


ROW_TILE = 512
FF_TILE = 512


def _rope_tables(pos, width):
    inv = 1.0 / (ROPE_THETA ** (jnp.arange(0, width, 2, dtype=F32) / width))
    ang = pos.astype(F32)[:, None] * inv
    cos, sin = jnp.cos(ang), jnp.sin(ang)
    reps = LANES // width
    return jnp.tile(jnp.concatenate([cos, cos], axis=1), (1, reps)), jnp.tile(jnp.concatenate([-sin, sin], axis=1), (1, reps))


def _prep_weights(w_in_ab, w_out_ab, mla_w_dq, mla_w_uq, mla_w_dkv, mla_w_uk, mla_w_uv, mla_w_o,
                  ffn_w_gate, ffn_w_up, ffn_w_down):
    bf = lambda w: w.astype(BF16)
    cols = (MOBA_HEADS * HEAD_DIM, MOBA_KV_HEADS * HEAD_DIM, MOBA_KV_HEADS * HEAD_DIM,
            DIFF_HEADS * 2 * DIFF_DIM, DIFF_KV_HEADS * 2 * DIFF_DIM, DIFF_KV_HEADS * 2 * DIFF_DIM)
    offs = [sum(cols[:i]) for i in range(len(cols) + 1)]
    w_in = bf(w_in_ab[0])
    w_out = bf(w_out_ab[0])
    uq = bf(mla_w_uq[0]).reshape(-1, MLA_HEADS, NOPE_DIM + ROPE_DIM)
    dkv = bf(mla_w_dkv[0])
    return dict(
        ab_in=[w_in[:, offs[i]:offs[i + 1]] for i in range(len(cols))],
        ab_out=[w_out[:MOBA_HEADS * HEAD_DIM], w_out[MOBA_HEADS * HEAD_DIM:]],
        dq=bf(mla_w_dq[0]),
        uq_nope=uq[:, :, :NOPE_DIM].reshape(-1, MLA_HEADS * NOPE_DIM),
        uq_rope=uq[:, :, NOPE_DIM:].reshape(-1, MLA_HEADS * ROPE_DIM),
        dkv_c=dkv[:, :KV_LORA],
        dkv_r2=jnp.concatenate([dkv[:, KV_LORA:], dkv[:, KV_LORA:]], axis=1),
        uk_t=bf(mla_w_uk[0]).transpose(1, 2, 0),
        uv_t=bf(mla_w_uv[0]).transpose(1, 0, 2),
        o=bf(mla_w_o[0]),
        gate=[bf(w) for w in ffn_w_gate], up=[bf(w) for w in ffn_w_up], down=[bf(w) for w in ffn_w_down],
    )


def _trunk(x3, pos0, w, prm, past):
    n_seq, t, d = x3.shape
    x = x3.reshape(n_seq * t, d)
    pos = jnp.tile(pos0 + jnp.arange(t, dtype=jnp.int32), n_seq)
    rope128 = _rope_tables(pos, HEAD_DIM)
    rope64 = _rope_tables(pos, DIFF_DIM)
    gains = prm["norm_gains"]
    lam_init0 = 0.8 - 0.6 * math.exp(-0.3 * 0)
    halo = CONV_W - 1

    def ffn(xin, layer):
        if past is None:
            past8 = jnp.zeros((n_seq, SUBLANES, D_FF), F32)
        else:
            past8 = jnp.pad(past["conv"][layer], ((0, 0), (SUBLANES - halo, 0), (0, 0)))
        return _ffn(xin, past8, gains[layer, 2], w["gate"][layer], w["up"][layer], prm["ffn_conv_w"][layer],
                    prm["ffn_conv_b"][layer], w["down"][layer], gains[layer, 3], t, ROW_TILE, FF_TILE)

    mq, mk, mv, dq, dk, dv = _proj(
        x, gains[0, 0], w["ab_in"], ["rope128", "rope128", "none", "rope64", "rope64", "none"],
        [rope128, rope128, (), rope64, rope64, ()], [wi.shape[1] for wi in w["ab_in"]], ROW_TILE // 2)
    if past is None:
        o_m = _moba_prefill(mq, mk, mv, n_seq, t)
        o_d = _diff_prefill(dq, dk, dv, prm["diff_lambda"][0], prm["diff_subln"][0], lam_init0, n_seq, t)
    else:
        pt = past["page_table"]
        by_row = lambda pool: pool[0].reshape(pool.shape[1], -1, LANES)
        by_feature = lambda pool: pool[0].reshape(pool.shape[1], PAGE_SIZE, -1).transpose(0, 2, 1)
        o_m = _moba_decode(mq, mk, mv, by_row(past["moba"][0]), by_row(past["moba"][1]), pt, t)
        o_d = _diff_decode(dq, dk, dv, by_feature(past["diff"][0]), by_row(past["diff"][1]), pt,
                           prm["diff_lambda"][0], prm["diff_subln"][0], lam_init0, t)
    x = _outproj([o_m, o_d], w["ab_out"], gains[0, 1], x, ROW_TILE)
    x, conv0 = ffn(x, 0)

    cq, ckv, kr2 = _proj(x, gains[1, 0], [w["dq"], w["dkv_c"], w["dkv_r2"]], ["rms", "rms", "rope64"],
                         [(prm["mla_g_q"][0],), (prm["mla_g_kv"][0],), rope64], [Q_LORA, KV_LORA, LANES], ROW_TILE)
    q_nope, q_rope = _proj(cq, None, [w["uq_nope"], w["uq_rope"]], ["none", "rope64"], [(), rope64],
                           [MLA_HEADS * NOPE_DIM, MLA_HEADS * ROPE_DIM], ROW_TILE)
    kr = kr2[:, :ROPE_DIM]
    if past is None:
        o = _mla_prefill(q_nope, q_rope, ckv, kr2, w["uk_t"], w["uv_t"], n_seq, t)
    else:
        q_rope_rows = q_rope.reshape(n_seq, t, MLA_HEADS, ROPE_DIM).transpose(0, 2, 1, 3).reshape(
            n_seq, MLA_HEADS * t, ROPE_DIM)
        o = _mla_decode(q_nope, q_rope_rows, ckv, kr, past["mla"][0][0], past["mla"][1][0].transpose(0, 2, 1),
                        past["page_table"], w["uk_t"], w["uv_t"], t)
    x = _outproj([o], [w["o"]], gains[1, 1], x, ROW_TILE)
    x, conv1 = ffn(x, 1)

    new = (mk.reshape(1, n_seq, t, MOBA_KV_HEADS, HEAD_DIM), mv.reshape(1, n_seq, t, MOBA_KV_HEADS, HEAD_DIM),
           dk.reshape(1, n_seq, t, DIFF_KV_HEADS, 2, DIFF_DIM), dv.reshape(1, n_seq, t, DIFF_KV_HEADS, 2 * DIFF_DIM),
           ckv.reshape(1, n_seq, t, KV_LORA), kr.reshape(1, n_seq, t, ROPE_DIM), jnp.stack([conv0, conv1]))
    return x.reshape(n_seq, t, d), new


def kernel(x_prompt, x_sample, cache_moba_k, cache_moba_v, cache_diff_k, cache_diff_v, cache_mla_ckv,
           cache_mla_krope, state_ffn_conv, page_table, norm_gains, w_in_ab, w_out_ab, diff_lambda, diff_subln,
           mla_w_dq, mla_g_q, mla_w_uq, mla_w_dkv, mla_g_kv, mla_w_uk, mla_w_uv, mla_w_o,
           ffn_w_gate, ffn_w_up, ffn_conv_w, ffn_conv_b, ffn_w_down):
    w = _prep_weights(w_in_ab, w_out_ab, mla_w_dq, mla_w_uq, mla_w_dkv, mla_w_uk, mla_w_uv, mla_w_o,
                      ffn_w_gate, ffn_w_up, ffn_w_down)
    prm = dict(norm_gains=norm_gains, diff_lambda=diff_lambda, diff_subln=diff_subln, mla_g_q=mla_g_q,
               mla_g_kv=mla_g_kv, ffn_conv_w=ffn_conv_w, ffn_conv_b=ffn_conv_b)
    past = dict(moba=(cache_moba_k, cache_moba_v), diff=(cache_diff_k, cache_diff_v),
                mla=(cache_mla_ckv, cache_mla_krope), conv=state_ffn_conv, page_table=page_table)
    y_prompt, new_p = _trunk(x_prompt, 0, w, prm, None)
    y_sample, new_s = _trunk(x_sample, PAST_LEN, w, prm, past)
    return (y_prompt, y_sample) + new_p + new_s
```

```python
import functools
import math

import jax
import jax.numpy as jnp
from jax import lax
from jax.experimental import pallas as pl
from jax.experimental.pallas import tpu as pltpu

D_MODEL = 2048
PAST_LEN = 8192
PAGE_SIZE = 128
HEAD_DIM = 128
MOBA_HEADS = 8
MOBA_KV_HEADS = 2
MOBA_BLOCK = 256
MOBA_TOPK = 3
DIFF_HEADS = 8
DIFF_KV_HEADS = 2
DIFF_DIM = 64
MLA_HEADS = 16
Q_LORA = 512
KV_LORA = 512
NOPE_DIM = 128
ROPE_DIM = 64
V_DIM = 128
D_FF = 5632
CONV_W = 3
ROPE_THETA = 10000.0
EPS = 1e-6

LANES = 128
SUBLANES = 8
VMEM_LIMIT = 56 * 1024 * 1024
NEG_INF = float("-inf")
BF16 = jnp.bfloat16
F32 = jnp.float32


def _cparams(sem):
    return pltpu.CompilerParams(dimension_semantics=sem, vmem_limit_bytes=VMEM_LIMIT)


def _rms(x, g):
    return x * lax.rsqrt(jnp.mean(x * x, axis=-1, keepdims=True) + EPS) * g


def _dot(a, b):
    return jnp.dot(a, b, preferred_element_type=F32)


def _dot_nt(a, b, precision=None):
    return lax.dot_general(a, b, (((1,), (1,)), ((), ())), preferred_element_type=F32, precision=precision)


def _rope_tile(y, cos, sin, width):
    if width == LANES:
        partner = pltpu.roll(y, LANES // 2, 1)
    else:
        lane = lax.broadcasted_iota(jnp.int32, y.shape, 1)
        half = width // 2
        partner = jnp.where(lane % width < half, pltpu.roll(y, LANES - half, 1), pltpu.roll(y, half, 1))
    return y * cos + partner * sin


def _proj_kernel(*refs, n_w, norm_in, epis, col_chunk):
    it = iter(refs)
    x_ref = next(it)
    g_ref = next(it) if norm_in else None
    w_refs = [next(it) for _ in range(n_w)]
    e_refs = []
    for e in epis:
        if e in ("rope128", "rope64"):
            e_refs.append((next(it), next(it)))
        elif e == "rms":
            e_refs.append((next(it),))
        else:
            e_refs.append(())
    o_refs = [next(it) for _ in range(n_w)]

    x = x_ref[...]
    if norm_in:
        x = _rms(x, g_ref[...])
    xb = x.astype(BF16)
    for w_ref, e, er, o_ref in zip(w_refs, epis, e_refs, o_refs):
        n = w_ref.shape[1]
        if e == "rms":
            o_ref[...] = _rms(_dot(xb, w_ref[...]), er[0][...])
            continue
        for c0 in range(0, n, col_chunk):
            c1 = min(n, c0 + col_chunk)
            y = _dot(xb, w_ref[:, c0:c1])
            if e == "none":
                o_ref[:, c0:c1] = y
            else:
                width = LANES if e == "rope128" else DIFF_DIM
                cos, sin = er[0][...], er[1][...]
                for t0 in range(0, c1 - c0, LANES):
                    o_ref[:, c0 + t0:c0 + t0 + LANES] = _rope_tile(y[:, t0:t0 + LANES], cos, sin, width)


def _proj(x, gain, weights, epis, extras, out_widths, tm):
    m, k = x.shape
    assert m % tm == 0
    row = lambda i: (i, 0)
    fixed = lambda i: (0, 0)
    args, specs = [x], [pl.BlockSpec((tm, k), row)]
    if gain is not None:
        args.append(gain.reshape(1, k))
        specs.append(pl.BlockSpec((1, k), fixed))
    for w in weights:
        args.append(w)
        specs.append(pl.BlockSpec(w.shape, fixed, pipeline_mode=pl.Buffered(1)))
    for e, ex in zip(epis, extras):
        if e in ("rope128", "rope64"):
            for t in ex:
                args.append(t)
                specs.append(pl.BlockSpec((tm, LANES), row))
        elif e == "rms":
            args.append(ex[0].reshape(1, -1))
            specs.append(pl.BlockSpec((1, ex[0].size), fixed))
    out_shape = [jax.ShapeDtypeStruct((m, n), F32) for n in out_widths]
    out_specs = [pl.BlockSpec((tm, n), row) for n in out_widths]
    kern = functools.partial(_proj_kernel, n_w=len(weights), norm_in=gain is not None, epis=tuple(epis),
                             col_chunk=512)
    return pl.pallas_call(kern, grid=(m // tm,), in_specs=specs, out_specs=out_specs, out_shape=out_shape,
                          compiler_params=_cparams(("parallel",)))(*args)


def _outproj_kernel(*refs, n_a):
    a_refs = refs[:n_a]
    w_refs = refs[n_a:2 * n_a]
    g_ref, x_ref, o_ref = refs[2 * n_a:]
    y = _dot(a_refs[0][...].astype(BF16), w_refs[0][...])
    for a_ref, w_ref in zip(a_refs[1:], w_refs[1:]):
        y = y + _dot(a_ref[...].astype(BF16), w_ref[...])
    o_ref[...] = x_ref[...] + _rms(y, g_ref[...])


def _outproj(acts, weights, gain, resid, tm):
    m, d = resid.shape
    row = lambda i: (i, 0)
    fixed = lambda i: (0, 0)
    specs = [pl.BlockSpec((tm, a.shape[1]), row) for a in acts]
    specs += [pl.BlockSpec(w.shape, fixed, pipeline_mode=pl.Buffered(1)) for w in weights]
    specs += [pl.BlockSpec((1, d), fixed), pl.BlockSpec((tm, d), row)]
    return pl.pallas_call(functools.partial(_outproj_kernel, n_a=len(acts)), grid=(m // tm,), in_specs=specs,
                          out_specs=pl.BlockSpec((tm, d), row), out_shape=jax.ShapeDtypeStruct((m, d), F32),
                          compiler_params=_cparams(("parallel",)))(*acts, *weights, gain.reshape(1, d), resid)


def _ffn_kernel(x_ref, g_in_ref, wg_ref, wu_ref, cw_ref, cb_ref, wd_ref, g_out_ref, past_ref,
                y_ref, tail_ref, h_s, acc_s, gbuf_s, carry_s, *, blocks_per_seq):
    i, f = pl.program_id(0), pl.program_id(1)
    n_seq, t_blk, tf = gbuf_s.shape[0], gbuf_s.shape[1] - SUBLANES, gbuf_s.shape[2]
    halo = CONV_W - 1

    @pl.when(f == 0)
    def _():
        h_s[...] = _rms(x_ref[...], g_in_ref[...]).astype(BF16)
        acc_s[...] = jnp.zeros_like(acc_s)
        if blocks_per_seq > 1:
            @pl.when(i == 0)
            def _():
                carry_s[...] = jnp.zeros_like(carry_s)

    h = h_s[...]
    g3 = _dot(h, wg_ref[...]).reshape(n_seq, t_blk, tf)
    u = _dot(h, wu_ref[...])

    if blocks_per_seq == 1:
        gbuf_s[:, 0:SUBLANES, :] = past_ref[...]
    else:
        gbuf_s[:, 0:SUBLANES, :] = jnp.where(i % blocks_per_seq == 0, past_ref[...], carry_s[f])
        carry_s[f] = g3[:, t_blk - SUBLANES:, :]
    gbuf_s[:, SUBLANES:, :] = g3
    tail_ref[...] = g3[:, t_blk - halo:, :]

    gc = cb_ref[...].reshape(1, 1, tf)
    for j in range(CONV_W):
        gc = gc + cw_ref[j:j + 1, :].reshape(1, 1, tf) * gbuf_s[:, SUBLANES - halo + j:SUBLANES - halo + j + t_blk, :]
    act = jax.nn.gelu(gc.reshape(n_seq * t_blk, tf), approximate=True) * u
    acc_s[...] += _dot(act.astype(BF16), wd_ref[...])

    @pl.when(f == pl.num_programs(1) - 1)
    def _():
        y_ref[...] = x_ref[...] + _rms(acc_s[...], g_out_ref[...])


def _ffn(x, past8, g_in, wg, wu, cw, cb, wd, g_out, seq_len, tm, tf):
    m, d = x.shape
    n_seq = m // seq_len
    if seq_len >= tm:
        blocks_per_seq, seq_per_blk, t_blk = seq_len // tm, 1, tm
    else:
        blocks_per_seq, seq_per_blk, t_blk = 1, tm // seq_len, seq_len
    n_f = D_FF // tf
    halo = CONV_W - 1
    seq_idx = lambda i, f: (i // blocks_per_seq, 0, f)
    in_specs = [
        pl.BlockSpec((tm, d), lambda i, f: (i, 0)),
        pl.BlockSpec((1, d), lambda i, f: (0, 0)),
        pl.BlockSpec((d, tf), lambda i, f: (0, f)),
        pl.BlockSpec((d, tf), lambda i, f: (0, f)),
        pl.BlockSpec((CONV_W, tf), lambda i, f: (0, f)),
        pl.BlockSpec((1, tf), lambda i, f: (0, f)),
        pl.BlockSpec((tf, d), lambda i, f: (f, 0)),
        pl.BlockSpec((1, d), lambda i, f: (0, 0)),
        pl.BlockSpec((seq_per_blk, SUBLANES, tf), seq_idx),
    ]
    out_specs = [pl.BlockSpec((tm, d), lambda i, f: (i, 0)),
                 pl.BlockSpec((seq_per_blk, halo, tf), lambda i, f: (i, 0, f))]
    out_shape = [jax.ShapeDtypeStruct((m, d), F32),
                 jax.ShapeDtypeStruct((m // tm * seq_per_blk, halo, D_FF), F32)]
    scratch = [pltpu.VMEM((tm, d), BF16), pltpu.VMEM((tm, d), F32),
               pltpu.VMEM((seq_per_blk, t_blk + SUBLANES, tf), F32),
               pltpu.VMEM((n_f, seq_per_blk, SUBLANES, tf), F32)]
    kern = functools.partial(_ffn_kernel, blocks_per_seq=blocks_per_seq)
    y, tails = pl.pallas_call(kern, grid=(m // tm, n_f), in_specs=in_specs, out_specs=out_specs,
                              out_shape=out_shape, scratch_shapes=scratch,
                              compiler_params=_cparams(("arbitrary", "arbitrary")))(
        x, g_in.reshape(1, d), wg, wu, cw, cb.reshape(1, D_FF), wd, g_out.reshape(1, d), past8)
    return y, tails[blocks_per_seq - 1::blocks_per_seq]


def _weighted_values(p, values):
    if not isinstance(values, (list, tuple)):
        return _dot(p, values)
    r = p.shape[0] // len(values)
    return jnp.concatenate([_dot(p[i * r:(i + 1) * r], v) for i, v in enumerate(values)], axis=0)


def _softmax_init(s, values, m_s, l_s, acc_s):
    m = jnp.max(s, axis=-1, keepdims=True)
    p = jnp.exp(s - m)
    m_s[...] = m
    l_s[...] = jnp.sum(p, axis=-1, keepdims=True)
    acc_s[...] = _weighted_values(p.astype(BF16), values)


def _softmax_update(s, values, m_s, l_s, acc_s):
    m_old = m_s[...]
    m_new = jnp.maximum(m_old, jnp.max(s, axis=-1, keepdims=True))
    alpha = jnp.exp(m_old - m_new)
    p = jnp.exp(s - m_new)
    m_s[...] = m_new
    l_s[...] = alpha * l_s[...] + jnp.sum(p, axis=-1, keepdims=True)
    acc_s[...] = alpha * acc_s[...] + _weighted_values(p.astype(BF16), values)


def _topk_mask(gate, valid, k, axis=1):
    lane = lax.broadcasted_iota(jnp.int32, gate.shape, axis)
    big = jnp.int32(gate.shape[axis])
    sel = jnp.zeros(gate.shape, jnp.bool_)
    rem = valid
    for _ in range(k):
        gm = jnp.where(rem, gate, NEG_INF)
        top = jnp.max(gm, axis=axis, keepdims=True)
        idx = jnp.min(jnp.where(rem & (gm == top), lane, big), axis=axis, keepdims=True)
        pick = lane == idx
        sel = sel | pick
        rem = rem & jnp.logical_not(pick)
    return sel


def _causal_tile(rows, cols, period):
    r = lax.broadcasted_iota(jnp.int32, (rows, cols), 0) % period
    c = lax.broadcasted_iota(jnp.int32, (rows, cols), 1)
    return c <= r


def _diff_lambda(lam_ref, lam_init):
    lv = lam_ref[...]
    d1 = jnp.sum(lv[0:1] * lv[1:2], axis=-1, keepdims=True)
    d2 = jnp.sum(lv[2:3] * lv[3:4], axis=-1, keepdims=True)
    return jnp.exp(d1) - jnp.exp(d2) + lam_init


def _split_maps(q2):
    lane = lax.broadcasted_iota(jnp.int32, q2.shape, 1)
    lo = jnp.where(lane < DIFF_DIM, q2, 0.0)
    hi = jnp.where(lane >= DIFF_DIM, q2, 0.0)
    return jnp.concatenate([lo, hi], axis=0)


TQ = MOBA_BLOCK
DIFF_HEAD_GROUP = 2
MLA_HEAD_GROUP = 4
LOG2_E = math.log2(math.e)


def _lane_halves_max(s):
    out = s[:, :LANES]
    for c0 in range(LANES, s.shape[1], LANES):
        out = jnp.maximum(out, s[:, c0:c0 + LANES])
    return out


def _lane_halves_sum(p):
    out = p[:, :LANES]
    for c0 in range(LANES, p.shape[1], LANES):
        out = out + p[:, c0:c0 + LANES]
    return out


def _two_pass_attend(qk, values, own_mask, past_bias, qi, scale, mx_s, l_s, acc_s):
    def raw_scores(j, k0):
        s = qk(k0)
        if past_bias is not None:
            s = s + jnp.concatenate([past_bias(j)] * (TQ // LANES), axis=1)
        return s

    own0 = pl.multiple_of(qi * TQ, TQ)
    s_own = jnp.where(own_mask, qk(own0), NEG_INF)
    mx_s[...] = _lane_halves_max(s_own)

    def pass1(j, carry):
        mx_s[...] = jnp.maximum(mx_s[...], _lane_halves_max(raw_scores(j, pl.multiple_of(j * TQ, TQ))))
        return carry

    lax.fori_loop(0, qi, pass1, 0)
    mx_s[...] = jnp.broadcast_to(jnp.max(mx_s[...], axis=-1, keepdims=True), mx_s.shape)

    def probs(s):
        m = jnp.concatenate([mx_s[...]] * (TQ // LANES), axis=1)
        return jnp.exp2((s - m) * (scale * LOG2_E))

    p = probs(s_own)
    l_s[...] = _lane_halves_sum(p)
    acc_s[...] = _weighted_values(p.astype(BF16), values(own0))

    def pass2(j, carry):
        k0 = pl.multiple_of(j * TQ, TQ)
        p = probs(raw_scores(j, k0))
        l_s[...] += _lane_halves_sum(p)
        acc_s[...] += _weighted_values(p.astype(BF16), values(k0))
        return carry

    lax.fori_loop(0, qi, pass2, 0)
    return acc_s[...] / jnp.sum(l_s[...], axis=-1, keepdims=True)


def _moba_prefill_kernel(q_ref, k_ref, v_ref, o_ref, kb_s, vb_s, bias_s, mx_s, l_s, acc_s):
    qi = pl.program_id(2)
    t = k_ref.shape[0]
    nb = t // MOBA_BLOCK
    n_sel = max(1, min(MOBA_TOPK, nb - 1))
    scale = HEAD_DIM ** -0.5
    g_per = MOBA_HEADS // MOBA_KV_HEADS

    @pl.when(qi == 0)
    def _():
        kb_s[...] = k_ref[...].astype(BF16)
        vb_s[...] = v_ref[...].astype(BF16)

    kmean = jnp.mean(k_ref[...].reshape(nb, MOBA_BLOCK, HEAD_DIM), axis=1)
    rows = g_per * TQ
    qf = _rows_by_head(q_ref, range(g_per), HEAD_DIM)
    qb = qf.astype(BF16)
    blk = lax.broadcasted_iota(jnp.int32, (nb, rows), 0)
    gate = _dot_nt(kmean, qf, lax.Precision.HIGHEST)
    sel = _topk_mask(gate, blk < qi, n_sel, axis=0).astype(F32)
    spread = (lax.broadcasted_iota(jnp.int32, (nb, nb * LANES), 1) // LANES
              == lax.broadcasted_iota(jnp.int32, (nb, nb * LANES), 0)).astype(F32)
    keep = lax.dot_general(sel, spread, (((0,), (0,)), ((), ())), preferred_element_type=F32)
    for j in range(nb):
        bias_s[j] = jnp.where(keep[:, j * LANES:(j + 1) * LANES] > 0.5, 0.0, NEG_INF)
    o = _two_pass_attend(lambda k0: _dot_nt(qb, kb_s[pl.ds(k0, TQ), :]), lambda k0: vb_s[pl.ds(k0, TQ), :],
                         _causal_tile(rows, TQ, TQ), lambda j: bias_s[j], qi, scale, mx_s, l_s, acc_s)
    for g in range(g_per):
        o_ref[:, g * HEAD_DIM:(g + 1) * HEAD_DIM] = o[g * TQ:(g + 1) * TQ]


def _moba_prefill(mq, mk, mv, n_batch, t):
    m = mq.shape[0]
    nq = t // TQ
    g_w = MOBA_HEADS // MOBA_KV_HEADS * HEAD_DIM
    q_spec = pl.BlockSpec((TQ, g_w), lambda b, k, i: (b * nq + i, k))
    kv_spec = pl.BlockSpec((t, HEAD_DIM), lambda b, k, i: (b, k))
    rows = MOBA_HEADS // MOBA_KV_HEADS * TQ
    scratch = [pltpu.VMEM((t, HEAD_DIM), BF16), pltpu.VMEM((t, HEAD_DIM), BF16),
               pltpu.VMEM((t // MOBA_BLOCK, rows, LANES), F32),
               pltpu.VMEM((rows, LANES), F32), pltpu.VMEM((rows, LANES), F32), pltpu.VMEM((rows, HEAD_DIM), F32)]
    return pl.pallas_call(_moba_prefill_kernel, grid=(n_batch, MOBA_KV_HEADS, nq),
                          in_specs=[q_spec, kv_spec, kv_spec], out_specs=q_spec,
                          out_shape=jax.ShapeDtypeStruct((m, MOBA_HEADS * HEAD_DIM), F32), scratch_shapes=scratch,
                          compiler_params=_cparams(("parallel", "parallel", "arbitrary")))(mq, mk, mv)


def _diff_finish(o2, lam, sub_g, lam_init, rows):
    o = o2[:rows] - lam * o2[rows:]
    return _rms(o, sub_g) * (1.0 - lam_init)


def _diff_prefill_kernel(q_ref, k_ref, v_ref, lam_ref, subg_ref, o_ref, kb_s, vb_s, mx_s, l_s, acc_s, *, lam_init):
    qi = pl.program_id(2)
    scale = DIFF_DIM ** -0.5
    g_per = DIFF_HEADS // DIFF_KV_HEADS
    width = 2 * DIFF_DIM

    @pl.when(qi == 0)
    def _():
        kb_s[...] = k_ref[...].astype(BF16)
        vb_s[...] = v_ref[...].astype(BF16)

    lam = _diff_lambda(lam_ref, lam_init)
    grp = mx_s.shape[0] // (2 * TQ)
    tril = _causal_tile(grp * 2 * TQ, TQ, TQ)
    for g0 in range(0, g_per, grp):
        qb = jnp.concatenate([_split_maps(q_ref[:, g * width:(g + 1) * width]) for g in range(g0, g0 + grp)],
                             axis=0).astype(BF16)
        o2 = _two_pass_attend(lambda k0: _dot_nt(qb, kb_s[pl.ds(k0, TQ), :]), lambda k0: vb_s[pl.ds(k0, TQ), :],
                              tril, None, qi, scale, mx_s, l_s, acc_s)
        for i in range(grp):
            g = g0 + i
            o_ref[:, g * width:(g + 1) * width] = _diff_finish(
                o2[i * 2 * TQ:(i + 1) * 2 * TQ], lam, subg_ref[...], lam_init, TQ)


def _diff_prefill(dq, dk, dv, lam_vec, sub_g, lam_init, n_batch, t):
    m = dq.shape[0]
    nq = t // TQ
    width = 2 * DIFF_DIM
    g_w = DIFF_HEADS // DIFF_KV_HEADS * width
    q_spec = pl.BlockSpec((TQ, g_w), lambda b, k, i: (b * nq + i, k))
    kv_spec = pl.BlockSpec((t, width), lambda b, k, i: (b, k))
    fixed = lambda b, k, i: (0, 0)
    rows = DIFF_HEAD_GROUP * 2 * TQ
    scratch = [pltpu.VMEM((t, width), BF16), pltpu.VMEM((t, width), BF16),
               pltpu.VMEM((rows, LANES), F32), pltpu.VMEM((rows, LANES), F32), pltpu.VMEM((rows, width), F32)]
    kern = functools.partial(_diff_prefill_kernel, lam_init=lam_init)
    return pl.pallas_call(kern, grid=(n_batch, DIFF_KV_HEADS, nq),
                          in_specs=[q_spec, kv_spec, kv_spec, pl.BlockSpec((4, DIFF_DIM), fixed),
                                    pl.BlockSpec((1, width), fixed)],
                          out_specs=q_spec, out_shape=jax.ShapeDtypeStruct((m, DIFF_HEADS * width), F32),
                          scratch_shapes=scratch,
                          compiler_params=_cparams(("parallel", "parallel", "arbitrary")))(
        dq, dk, dv, lam_vec, sub_g.reshape(1, width))


def _mla_prefill_kernel(qn_ref, qr_ref, ckv_ref, kr2_ref, wuk_ref, wuv_ref, o_ref, kx_s, vx_s, mx_s, l_s, acc_s):
    qi = pl.program_id(2)
    grp = kx_s.shape[0]
    scale = (NOPE_DIM + ROPE_DIM) ** -0.5

    @pl.when(qi == 0)
    def _():
        cb = ckv_ref[...].astype(BF16)
        kr = kr2_ref[...].astype(BF16)
        for i in range(grp):
            kx_s[i, :, :NOPE_DIM] = _dot_nt(cb, wuk_ref[i]).astype(BF16)
            kx_s[i, :, NOPE_DIM:] = kr
            vx_s[i] = _dot(cb, wuv_ref[i]).astype(BF16)

    lane = lax.broadcasted_iota(jnp.int32, (TQ, LANES), 1)
    q_heads = []
    for i in range(grp):
        pair = qr_ref[:, (i // 2) * LANES:(i // 2 + 1) * LANES]
        in_head = (lane < ROPE_DIM) if i % 2 == 0 else (lane >= ROPE_DIM)
        q_r = jnp.where(in_head, pair, 0.0)
        q_heads.append(jnp.concatenate([qn_ref[:, i * NOPE_DIM:(i + 1) * NOPE_DIM], q_r], axis=1).astype(BF16))

    def qk(k0):
        return jnp.concatenate([_dot_nt(q_heads[i], kx_s[i, pl.ds(k0, TQ), :]) for i in range(grp)], axis=0)

    o = _two_pass_attend(qk, lambda k0: [vx_s[i, pl.ds(k0, TQ), :] for i in range(grp)],
                         _causal_tile(grp * TQ, TQ, TQ), None, qi, scale, mx_s, l_s, acc_s)
    for i in range(grp):
        o_ref[:, i * V_DIM:(i + 1) * V_DIM] = o[i * TQ:(i + 1) * TQ]


def _mla_prefill(q_nope, q_rope, ckv, kr2, wuk_t, wuv_t, n_batch, t):
    m = q_nope.shape[0]
    nq = t // TQ
    grp = MLA_HEAD_GROUP
    row = lambda b, g, i: (b * nq + i, g)
    seq = lambda b, g, i: (b, 0)
    head = lambda b, g, i: (g, 0, 0)
    scratch = [pltpu.VMEM((grp, t, NOPE_DIM + LANES), BF16), pltpu.VMEM((grp, t, V_DIM), BF16),
               pltpu.VMEM((grp * TQ, LANES), F32), pltpu.VMEM((grp * TQ, LANES), F32),
               pltpu.VMEM((grp * TQ, V_DIM), F32)]
    in_specs = [pl.BlockSpec((TQ, grp * NOPE_DIM), row), pl.BlockSpec((TQ, grp * ROPE_DIM), row),
                pl.BlockSpec((t, KV_LORA), seq), pl.BlockSpec((t, LANES), seq),
                pl.BlockSpec((grp,) + wuk_t.shape[1:], head), pl.BlockSpec((grp,) + wuv_t.shape[1:], head)]
    return pl.pallas_call(_mla_prefill_kernel, grid=(n_batch, MLA_HEADS // grp, nq), in_specs=in_specs,
                          out_specs=pl.BlockSpec((TQ, grp * V_DIM), row),
                          out_shape=jax.ShapeDtypeStruct((m, MLA_HEADS * V_DIM), F32), scratch_shapes=scratch,
                          compiler_params=_cparams(("parallel", "parallel", "arbitrary")))(
        q_nope, q_rope, ckv, kr2, wuk_t, wuv_t)


PAGES_PER_STEP = 16
N_PAGES = PAST_LEN // PAGE_SIZE


def _page_copy(pt_ref, pool, buf, sem, step, slot, u, pages_per_step):
    return pltpu.make_async_copy(pool.at[pt_ref[step * pages_per_step + u]], buf.at[slot, u], sem.at[slot])


PAGE_BUFFERS = 3


def _fetch_pages(pt_ref, pools, bufs, sems, pages_per_step):
    n_chunk = pl.num_programs(1)
    n_step = pl.num_programs(0) * n_chunk
    g = pl.program_id(0) * n_chunk + pl.program_id(1)
    ahead = PAGE_BUFFERS - 1

    def copies(step):
        slot = step % PAGE_BUFFERS
        return [_page_copy(pt_ref, pool, buf, sems.at[i], step, slot, u, pages_per_step)
                for u in range(pages_per_step) for i, (pool, buf) in enumerate(zip(pools, bufs))]

    for first in range(ahead):
        @pl.when(jnp.logical_and(g == 0, first < n_step))
        def _():
            for cp in copies(first):
                cp.start()

    @pl.when(g + ahead < n_step)
    def _():
        for cp in copies(g + ahead):
            cp.start()

    for cp in copies(g):
        cp.wait()
    slot = g % PAGE_BUFFERS
    return [[buf.at[slot, u] for u in range(pages_per_step)] for buf in bufs]


def _paged_operands(pools, pages_per_step):
    specs = [pl.BlockSpec(memory_space=pl.ANY) for _ in pools]
    bufs = [pltpu.VMEM((PAGE_BUFFERS, pages_per_step) + pool.shape[1:], pool.dtype) for pool in pools]
    return specs, bufs, pltpu.SemaphoreType.DMA((len(pools), PAGE_BUFFERS))


def _head_rows(refs, kvh, n_kvh):
    return jnp.concatenate([r[pl.ds(kvh, PAGE_SIZE, stride=n_kvh), :] for r in refs], axis=0)


def _rows_by_head(ref, heads, width):
    return jnp.concatenate([ref[:, h * width:(h + 1) * width] for h in heads], axis=0)


def _moba_decode_kernel(pt_ref, q_ref, kn_ref, vn_ref, k_pool, v_pool, o_ref, m_s, l_s, o_s, km_s,
                        k_buf, v_buf, sems, *, pages_per_step, t_new):
    p = pages_per_step
    k_refs, v_refs = _fetch_pages(pt_ref, (k_pool, v_pool), (k_buf, v_buf), sems, p)
    c = pl.program_id(1)
    scale = HEAD_DIM ** -0.5
    g_per = MOBA_HEADS // MOBA_KV_HEADS
    rows = g_per * t_new
    pages_per_blk = MOBA_BLOCK // PAGE_SIZE
    blk_per_step = p // pages_per_blk
    n_past_blk = PAST_LEN // MOBA_BLOCK
    n_sel = max(1, min(MOBA_TOPK, n_past_blk))

    lane = lax.broadcasted_iota(jnp.int32, (rows, LANES), 1)

    @pl.when(c == 0)
    def _():
        m_s[...] = jnp.zeros_like(m_s)
        l_s[...] = jnp.zeros_like(l_s)

    for kvh in range(MOBA_KV_HEADS):
        q_rows = _rows_by_head(q_ref, range(kvh * g_per, (kvh + 1) * g_per), HEAD_DIM)
        qb = q_rows.astype(BF16)
        kf = _head_rows(k_refs, kvh, MOBA_KV_HEADS)
        vb = _head_rows(v_refs, kvh, MOBA_KV_HEADS).astype(BF16)
        s_all = _dot_nt(qb, kf.astype(BF16)) * scale
        m_mat, l_mat = m_s[kvh], l_s[kvh]
        for b in range(blk_per_step):
            k0, k1 = b * MOBA_BLOCK, (b + 1) * MOBA_BLOCK
            s = s_all[:, k0:k1]
            m = jnp.max(s, axis=-1, keepdims=True)
            e = jnp.exp(s - m)
            blk = c * blk_per_step + b
            m_mat = jnp.where(lane == blk, m, m_mat)
            l_mat = jnp.where(lane == blk, jnp.sum(e, axis=-1, keepdims=True), l_mat)
            o_s[kvh, blk] = _dot(e.astype(BF16), vb[k0:k1])
            km_s[kvh, pl.ds(blk, 1), :] = jnp.mean(kf[k0:k1], axis=0, keepdims=True)
        m_s[kvh] = m_mat
        l_s[kvh] = l_mat

    @pl.when(c == pl.num_programs(1) - 1)
    def _():
        causal = _causal_tile(MOBA_KV_HEADS * rows, t_new, t_new)
        q_k = [_rows_by_head(q_ref, range(k * g_per, (k + 1) * g_per), HEAD_DIM) for k in range(MOBA_KV_HEADS)]
        gate = jnp.concatenate([_dot_nt(q_k[k], km_s[k], lax.Precision.HIGHEST) for k in range(MOBA_KV_HEADS)],
                               axis=0)
        sel = _topk_mask(gate, jnp.ones(gate.shape, jnp.bool_), n_sel)
        s_own = jnp.concatenate(
            [_dot_nt(q_k[k].astype(BF16), kn_ref[:, k * HEAD_DIM:(k + 1) * HEAD_DIM].astype(BF16))
             for k in range(MOBA_KV_HEADS)], axis=0) * scale
        s_own = jnp.where(causal, s_own, NEG_INF)
        m_blk = jnp.concatenate([m_s[k] for k in range(MOBA_KV_HEADS)], axis=0)[:, :n_past_blk]
        l_blk = jnp.concatenate([l_s[k] for k in range(MOBA_KV_HEADS)], axis=0)[:, :n_past_blk]
        m_fin = jnp.maximum(jnp.max(s_own, axis=-1, keepdims=True),
                            jnp.max(jnp.where(sel, m_blk, NEG_INF), axis=-1, keepdims=True))
        e_own = jnp.exp(s_own - m_fin)
        w = jnp.exp(jnp.where(sel, m_blk - m_fin, NEG_INF))
        l_fin = jnp.sum(e_own, axis=-1, keepdims=True) + jnp.sum(w * l_blk, axis=-1, keepdims=True)
        spread = (lax.broadcasted_iota(jnp.int32, (n_past_blk, n_past_blk * LANES), 1) // LANES
                  == lax.broadcasted_iota(jnp.int32, (n_past_blk, n_past_blk * LANES), 0)).astype(BF16)
        w_head = w.astype(BF16)
        w_lanes = _dot(w_head, spread) + _dot((w - w_head.astype(F32)).astype(BF16), spread)
        for k in range(MOBA_KV_HEADS):
            r0, r1 = k * rows, (k + 1) * rows
            parts = [_dot(e_own[r0:r1].astype(BF16), vn_ref[:, k * HEAD_DIM:(k + 1) * HEAD_DIM].astype(BF16)),
                     jnp.zeros((rows, HEAD_DIM), F32)]
            for b in range(n_past_blk):
                parts[b % 2] = parts[b % 2] + w_lanes[r0:r1, b * LANES:(b + 1) * LANES] * o_s[k, b]
            o = (parts[0] + parts[1]) / l_fin[r0:r1]
            for i in range(g_per):
                h = k * g_per + i
                o_ref[:, h * HEAD_DIM:(h + 1) * HEAD_DIM] = o[i * t_new:(i + 1) * t_new]


def _moba_decode(mq, mk, mv, pool_k, pool_v, page_table, t_new):
    n_seq = mq.shape[0] // t_new
    p = PAGES_PER_STEP
    kv_w = MOBA_KV_HEADS * HEAD_DIM
    g_per = MOBA_HEADS // MOBA_KV_HEADS
    rows = g_per * t_new
    n_blk = PAST_LEN // MOBA_BLOCK
    seq = lambda s, c, pt: (s, 0)
    pool_specs, bufs, sems = _paged_operands((pool_k, pool_v), p)
    in_specs = [pl.BlockSpec((t_new, MOBA_HEADS * HEAD_DIM), seq), pl.BlockSpec((t_new, kv_w), seq),
                pl.BlockSpec((t_new, kv_w), seq)] + pool_specs
    assert n_blk <= LANES
    scratch = [pltpu.VMEM((MOBA_KV_HEADS, rows, LANES), F32), pltpu.VMEM((MOBA_KV_HEADS, rows, LANES), F32),
               pltpu.VMEM((MOBA_KV_HEADS, n_blk, rows, HEAD_DIM), F32), pltpu.VMEM((MOBA_KV_HEADS, n_blk, HEAD_DIM), F32)]
    grid_spec = pltpu.PrefetchScalarGridSpec(
        num_scalar_prefetch=1, grid=(n_seq, N_PAGES // p), in_specs=in_specs,
        out_specs=pl.BlockSpec((t_new, MOBA_HEADS * HEAD_DIM), seq), scratch_shapes=scratch + bufs + [sems])
    kern = functools.partial(_moba_decode_kernel, pages_per_step=p, t_new=t_new)
    return pl.pallas_call(kern, grid_spec=grid_spec, out_shape=jax.ShapeDtypeStruct(mq.shape, F32),
                          compiler_params=_cparams(("arbitrary", "arbitrary")))(
        page_table.reshape(-1), mq, mk, mv, pool_k, pool_v)


def _diff_decode_kernel(pt_ref, q_ref, kn_ref, vn_ref, lam_ref, subg_ref, k_pool, v_pool, o_ref, m_s, l_s, acc_s,
                        k_buf, v_buf, sems, *, pages_per_step, t_new, lam_init):
    p = pages_per_step
    k_refs, v_refs = _fetch_pages(pt_ref, (k_pool, v_pool), (k_buf, v_buf), sems, p)
    c = pl.program_id(1)
    scale = DIFF_DIM ** -0.5
    g_per = DIFF_HEADS // DIFF_KV_HEADS
    width = 2 * DIFF_DIM
    rows = g_per * t_new
    kvhs = range(DIFF_KV_HEADS)
    qb = [_split_maps(_rows_by_head(q_ref, range(k * g_per, (k + 1) * g_per), width)).astype(BF16) for k in kvhs]

    @pl.when(c == 0)
    def _():
        causal = _causal_tile(DIFF_KV_HEADS * 2 * rows, t_new, t_new)
        s_own = jnp.concatenate([_dot_nt(qb[k], kn_ref[:, k * width:(k + 1) * width].astype(BF16)) for k in kvhs],
                                axis=0) * scale
        _softmax_init(jnp.where(causal, s_own, NEG_INF),
                      [vn_ref[:, k * width:(k + 1) * width].astype(BF16) for k in kvhs], m_s, l_s, acc_s)

    s = jnp.concatenate(
        [_dot(qb[k], jnp.concatenate([r[k * width:(k + 1) * width, :] for r in k_refs], axis=1).astype(BF16))
         for k in kvhs], axis=0) * scale
    _softmax_update(s, [_head_rows(v_refs, k, DIFF_KV_HEADS).astype(BF16) for k in kvhs], m_s, l_s, acc_s)

    @pl.when(c == pl.num_programs(1) - 1)
    def _():
        lam = _diff_lambda(lam_ref, lam_init)
        o2 = acc_s[...] / l_s[...]
        for k in kvhs:
            o = _diff_finish(o2[k * 2 * rows:(k + 1) * 2 * rows], lam, subg_ref[...], lam_init, rows)
            for i in range(g_per):
                h = k * g_per + i
                o_ref[:, h * width:(h + 1) * width] = o[i * t_new:(i + 1) * t_new]


def _diff_decode(dq, dk, dv, pool_k, pool_v, page_table, lam_vec, sub_g, lam_init, t_new):
    n_seq = dq.shape[0] // t_new
    p = PAGES_PER_STEP
    width = 2 * DIFF_DIM
    kv_w = DIFF_KV_HEADS * width
    rows2 = 2 * (DIFF_HEADS // DIFF_KV_HEADS) * t_new
    seq = lambda s, c, pt: (s, 0)
    fixed = lambda s, c, pt: (0, 0)
    in_specs = [pl.BlockSpec((t_new, DIFF_HEADS * width), seq), pl.BlockSpec((t_new, kv_w), seq),
                pl.BlockSpec((t_new, kv_w), seq), pl.BlockSpec((4, DIFF_DIM), fixed),
                pl.BlockSpec((1, width), fixed)]
    pool_specs, bufs, sems = _paged_operands((pool_k, pool_v), p)
    scratch = [pltpu.VMEM((DIFF_KV_HEADS * rows2, 1), F32), pltpu.VMEM((DIFF_KV_HEADS * rows2, 1), F32),
               pltpu.VMEM((DIFF_KV_HEADS * rows2, width), F32)]
    grid_spec = pltpu.PrefetchScalarGridSpec(
        num_scalar_prefetch=1, grid=(n_seq, N_PAGES // p), in_specs=in_specs + pool_specs,
        out_specs=pl.BlockSpec((t_new, DIFF_HEADS * width), seq), scratch_shapes=scratch + bufs + [sems])
    kern = functools.partial(_diff_decode_kernel, pages_per_step=p, t_new=t_new, lam_init=lam_init)
    return pl.pallas_call(kern, grid_spec=grid_spec, out_shape=jax.ShapeDtypeStruct(dq.shape, F32),
                          compiler_params=_cparams(("arbitrary", "arbitrary")))(
        page_table.reshape(-1), dq, dk, dv, lam_vec, sub_g.reshape(1, width), pool_k, pool_v)


def _mla_decode_kernel(pt_ref, qn_ref, qr_ref, cn_ref, krn_ref, wuk_ref, wuv_ref, c_pool, r_pool, o_ref,
                       qlat_s, m_s, l_s, acc_s, c_buf, r_buf, sems, *, pages_per_step, t_new):
    p = pages_per_step
    c_refs, r_refs = _fetch_pages(pt_ref, (c_pool, r_pool), (c_buf, r_buf), sems, p)
    c = pl.program_id(1)
    scale = (NOPE_DIM + ROPE_DIM) ** -0.5
    rows = MLA_HEADS * t_new

    @pl.when(c == 0)
    def _():
        for h in range(MLA_HEADS):
            qlat_s[h * t_new:(h + 1) * t_new, :] = _dot(
                qn_ref[:, h * NOPE_DIM:(h + 1) * NOPE_DIM].astype(BF16), wuk_ref[h])
        q_lat = qlat_s[...].astype(BF16)
        causal = _causal_tile(rows, t_new, t_new)
        cn = cn_ref[...].astype(BF16)
        s_own = (_dot_nt(q_lat, cn) + _dot_nt(qr_ref[...].astype(BF16), krn_ref[...].astype(BF16))) * scale
        _softmax_init(jnp.where(causal, s_own, NEG_INF), cn, m_s, l_s, acc_s)

    q_lat = qlat_s[...].astype(BF16)
    ckv = jnp.concatenate([r[...] for r in c_refs], axis=0).astype(BF16)
    kr_t = jnp.concatenate([r[...] for r in r_refs], axis=1).astype(BF16)
    s = (_dot_nt(q_lat, ckv) + _dot(qr_ref[...].astype(BF16), kr_t)) * scale
    _softmax_update(s, ckv, m_s, l_s, acc_s)

    @pl.when(c == pl.num_programs(1) - 1)
    def _():
        o_lat = (acc_s[...] / l_s[...]).astype(BF16)
        for h in range(MLA_HEADS):
            o_ref[:, h * V_DIM:(h + 1) * V_DIM] = _dot(o_lat[h * t_new:(h + 1) * t_new], wuv_ref[h])


def _mla_decode(q_nope, q_rope_rows, ckv_new, kr_new, pool_c, pool_r, page_table, wuk_t, wuv_t, t_new):
    n_seq = q_nope.shape[0] // t_new
    p = PAGES_PER_STEP
    rows = MLA_HEADS * t_new
    seq = lambda s, c, pt: (s, 0)
    fixed3 = lambda s, c, pt: (0, 0, 0)
    in_specs = [pl.BlockSpec((t_new, MLA_HEADS * NOPE_DIM), seq),
                pl.BlockSpec((None, rows, ROPE_DIM), lambda s, c, pt: (s, 0, 0)),
                pl.BlockSpec((t_new, KV_LORA), seq), pl.BlockSpec((t_new, ROPE_DIM), seq),
                pl.BlockSpec(wuk_t.shape, fixed3, pipeline_mode=pl.Buffered(1)),
                pl.BlockSpec(wuv_t.shape, fixed3, pipeline_mode=pl.Buffered(1))]
    pool_specs, bufs, sems = _paged_operands((pool_c, pool_r), p)
    scratch = [pltpu.VMEM((rows, KV_LORA), F32), pltpu.VMEM((rows, 1), F32), pltpu.VMEM((rows, 1), F32),
               pltpu.VMEM((rows, KV_LORA), F32)]
    grid_spec = pltpu.PrefetchScalarGridSpec(
        num_scalar_prefetch=1, grid=(n_seq, N_PAGES // p), in_specs=in_specs + pool_specs,
        out_specs=pl.BlockSpec((t_new, MLA_HEADS * V_DIM), seq), scratch_shapes=scratch + bufs + [sems])
    kern = functools.partial(_mla_decode_kernel, pages_per_step=p, t_new=t_new)
    return pl.pallas_call(kern, grid_spec=grid_spec,
                          out_shape=jax.ShapeDtypeStruct((n_seq * t_new, MLA_HEADS * V_DIM), F32),
                          compiler_params=_cparams(("arbitrary", "arbitrary")))(
        page_table.reshape(-1), q_nope, q_rope_rows, ckv_new, kr_new, wuk_t, wuv_t, pool_c, pool_r)


ROW_TILE = 512
FF_TILE = 512


def _rope_tables(pos, width):
    inv = 1.0 / (ROPE_THETA ** (jnp.arange(0, width, 2, dtype=F32) / width))
    ang = pos.astype(F32)[:, None] * inv
    cos, sin = jnp.cos(ang), jnp.sin(ang)
    reps = LANES // width
    return jnp.tile(jnp.concatenate([cos, cos], axis=1), (1, reps)), jnp.tile(jnp.concatenate([-sin, sin], axis=1), (1, reps))


def _prep_weights(w_in_ab, w_out_ab, mla_w_dq, mla_w_uq, mla_w_dkv, mla_w_uk, mla_w_uv, mla_w_o,
                  ffn_w_gate, ffn_w_up, ffn_w_down):
    bf = lambda w: w.astype(BF16)
    cols = (MOBA_HEADS * HEAD_DIM, MOBA_KV_HEADS * HEAD_DIM, MOBA_KV_HEADS * HEAD_DIM,
            DIFF_HEADS * 2 * DIFF_DIM, DIFF_KV_HEADS * 2 * DIFF_DIM, DIFF_KV_HEADS * 2 * DIFF_DIM)
    offs = [sum(cols[:i]) for i in range(len(cols) + 1)]
    w_in = bf(w_in_ab[0])
    w_out = bf(w_out_ab[0])
    uq = bf(mla_w_uq[0]).reshape(-1, MLA_HEADS, NOPE_DIM + ROPE_DIM)
    dkv = bf(mla_w_dkv[0])
    return dict(
        ab_in=[w_in[:, offs[i]:offs[i + 1]] for i in range(len(cols))],
        ab_out=[w_out[:MOBA_HEADS * HEAD_DIM], w_out[MOBA_HEADS * HEAD_DIM:]],
        dq=bf(mla_w_dq[0]),
        uq_nope=uq[:, :, :NOPE_DIM].reshape(-1, MLA_HEADS * NOPE_DIM),
        uq_rope=uq[:, :, NOPE_DIM:].reshape(-1, MLA_HEADS * ROPE_DIM),
        dkv_c=dkv[:, :KV_LORA],
        dkv_r2=jnp.concatenate([dkv[:, KV_LORA:], dkv[:, KV_LORA:]], axis=1),
        uk_t=bf(mla_w_uk[0]).transpose(1, 2, 0),
        uv_t=bf(mla_w_uv[0]).transpose(1, 0, 2),
        o=bf(mla_w_o[0]),
        gate=[bf(w) for w in ffn_w_gate], up=[bf(w) for w in ffn_w_up], down=[bf(w) for w in ffn_w_down],
    )


def _trunk(x3, pos0, w, prm, past):
    n_seq, t, d = x3.shape
    x = x3.reshape(n_seq * t, d)
    pos = jnp.tile(pos0 + jnp.arange(t, dtype=jnp.int32), n_seq)
    rope128 = _rope_tables(pos, HEAD_DIM)
    rope64 = _rope_tables(pos, DIFF_DIM)
    gains = prm["norm_gains"]
    lam_init0 = 0.8 - 0.6 * math.exp(-0.3 * 0)
    halo = CONV_W - 1

    def ffn(xin, layer):
        if past is None:
            past8 = jnp.zeros((n_seq, SUBLANES, D_FF), F32)
        else:
            past8 = jnp.pad(past["conv"][layer], ((0, 0), (SUBLANES - halo, 0), (0, 0)))
        return _ffn(xin, past8, gains[layer, 2], w["gate"][layer], w["up"][layer], prm["ffn_conv_w"][layer],
                    prm["ffn_conv_b"][layer], w["down"][layer], gains[layer, 3], t, ROW_TILE, FF_TILE)

    mq, mk, mv, dq, dk, dv = _proj(
        x, gains[0, 0], w["ab_in"], ["rope128", "rope128", "none", "rope64", "rope64", "none"],
        [rope128, rope128, (), rope64, rope64, ()], [wi.shape[1] for wi in w["ab_in"]], ROW_TILE // 2)
    if past is None:
        o_m = _moba_prefill(mq, mk, mv, n_seq, t)
        o_d = _diff_prefill(dq, dk, dv, prm["diff_lambda"][0], prm["diff_subln"][0], lam_init0, n_seq, t)
    else:
        pt = past["page_table"]
        by_row = lambda pool: pool[0].reshape(pool.shape[1], -1, LANES)
        by_feature = lambda pool: pool[0].reshape(pool.shape[1], PAGE_SIZE, -1).transpose(0, 2, 1)
        o_m = _moba_decode(mq, mk, mv, by_row(past["moba"][0]), by_row(past["moba"][1]), pt, t)
        o_d = _diff_decode(dq, dk, dv, by_feature(past["diff"][0]), by_row(past["diff"][1]), pt,
                           prm["diff_lambda"][0], prm["diff_subln"][0], lam_init0, t)
    x = _outproj([o_m, o_d], w["ab_out"], gains[0, 1], x, ROW_TILE)
    x, conv0 = ffn(x, 0)

    cq, ckv, kr2 = _proj(x, gains[1, 0], [w["dq"], w["dkv_c"], w["dkv_r2"]], ["rms", "rms", "rope64"],
                         [(prm["mla_g_q"][0],), (prm["mla_g_kv"][0],), rope64], [Q_LORA, KV_LORA, LANES], ROW_TILE)
    q_nope, q_rope = _proj(cq, None, [w["uq_nope"], w["uq_rope"]], ["none", "rope64"], [(), rope64],
                           [MLA_HEADS * NOPE_DIM, MLA_HEADS * ROPE_DIM], ROW_TILE)
    kr = kr2[:, :ROPE_DIM]
    if past is None:
        o = _mla_prefill(q_nope, q_rope, ckv, kr2, w["uk_t"], w["uv_t"], n_seq, t)
    else:
        q_rope_rows = q_rope.reshape(n_seq, t, MLA_HEADS, ROPE_DIM).transpose(0, 2, 1, 3).reshape(
            n_seq, MLA_HEADS * t, ROPE_DIM)
        o = _mla_decode(q_nope, q_rope_rows, ckv, kr, past["mla"][0][0], past["mla"][1][0].transpose(0, 2, 1),
                        past["page_table"], w["uk_t"], w["uv_t"], t)
    x = _outproj([o], [w["o"]], gains[1, 1], x, ROW_TILE)
    x, conv1 = ffn(x, 1)

    new = (mk.reshape(1, n_seq, t, MOBA_KV_HEADS, HEAD_DIM), mv.reshape(1, n_seq, t, MOBA_KV_HEADS, HEAD_DIM),
           dk.reshape(1, n_seq, t, DIFF_KV_HEADS, 2, DIFF_DIM), dv.reshape(1, n_seq, t, DIFF_KV_HEADS, 2 * DIFF_DIM),
           ckv.reshape(1, n_seq, t, KV_LORA), kr.reshape(1, n_seq, t, ROPE_DIM), jnp.stack([conv0, conv1]))
    return x.reshape(n_seq, t, d), new


def kernel(x_prompt, x_sample, cache_moba_k, cache_moba_v, cache_diff_k, cache_diff_v, cache_mla_ckv,
           cache_mla_krope, state_ffn_conv, page_table, norm_gains, w_in_ab, w_out_ab, diff_lambda, diff_subln,
           mla_w_dq, mla_g_q, mla_w_uq, mla_w_dkv, mla_g_kv, mla_w_uk, mla_w_uv, mla_w_o,
           ffn_w_gate, ffn_w_up, ffn_conv_w, ffn_conv_b, ffn_w_down):
    w = _prep_weights(w_in_ab, w_out_ab, mla_w_dq, mla_w_uq, mla_w_dkv, mla_w_uk, mla_w_uv, mla_w_o,
                      ffn_w_gate, ffn_w_up, ffn_w_down)
    prm = dict(norm_gains=norm_gains, diff_lambda=diff_lambda, diff_subln=diff_subln, mla_g_q=mla_g_q,
               mla_g_kv=mla_g_kv, ffn_conv_w=ffn_conv_w, ffn_conv_b=ffn_conv_b)
    past = dict(moba=(cache_moba_k, cache_moba_v), diff=(cache_diff_k, cache_diff_v),
                mla=(cache_mla_ckv, cache_mla_krope), conv=state_ffn_conv, page_table=page_table)
    y_prompt, new_p = _trunk(x_prompt, 0, w, prm, None)
    y_sample, new_s = _trunk(x_sample, PAST_LEN, w, prm, past)
    return (y_prompt, y_sample) + new_p + new_s
```

```python
import functools
import math

import jax
import jax.numpy as jnp
from jax import lax
from jax.experimental import pallas as pl
from jax.experimental.pallas import tpu as pltpu

D_MODEL = 2048
PAST_LEN = 8192
PAGE_SIZE = 128
HEAD_DIM = 128
MOBA_HEADS = 8
MOBA_KV_HEADS = 2
MOBA_BLOCK = 256
MOBA_TOPK = 3
DIFF_HEADS = 8
DIFF_KV_HEADS = 2
DIFF_DIM = 64
MLA_HEADS = 16
Q_LORA = 512
KV_LORA = 512
NOPE_DIM = 128
ROPE_DIM = 64
V_DIM = 128
D_FF = 5632
CONV_W = 3
ROPE_THETA = 10000.0
EPS = 1e-6

LANES = 128
SUBLANES = 8
VMEM_LIMIT = 56 * 1024 * 1024
NEG_INF = float("-inf")
BF16 = jnp.bfloat16
F32 = jnp.float32


def _cparams(sem):
    return pltpu.CompilerParams(dimension_semantics=sem, vmem_limit_bytes=VMEM_LIMIT)


def _rms(x, g):
    return x * lax.rsqrt(jnp.mean(x * x, axis=-1, keepdims=True) + EPS) * g


def _dot(a, b):
    return jnp.dot(a, b, preferred_element_type=F32)


def _dot_nt(a, b, precision=None):
    return lax.dot_general(a, b, (((1,), (1,)), ((), ())), preferred_element_type=F32, precision=precision)


def _rope_tile(y, cos, sin, width):
    if width == LANES:
        partner = pltpu.roll(y, LANES // 2, 1)
    else:
        lane = lax.broadcasted_iota(jnp.int32, y.shape, 1)
        half = width // 2
        partner = jnp.where(lane % width < half, pltpu.roll(y, LANES - half, 1), pltpu.roll(y, half, 1))
    return y * cos + partner * sin


def _proj_kernel(*refs, n_w, norm_in, epis, col_chunk):
    it = iter(refs)
    x_ref = next(it)
    g_ref = next(it) if norm_in else None
    w_refs = [next(it) for _ in range(n_w)]
    e_refs = []
    for e in epis:
        if e in ("rope128", "rope64"):
            e_refs.append((next(it), next(it)))
        elif e == "rms":
            e_refs.append((next(it),))
        else:
            e_refs.append(())
    o_refs = [next(it) for _ in range(n_w)]

    x = x_ref[...]
    if norm_in:
        x = _rms(x, g_ref[...])
    xb = x.astype(BF16)
    for w_ref, e, er, o_ref in zip(w_refs, epis, e_refs, o_refs):
        n = w_ref.shape[1]
        if e == "rms":
            o_ref[...] = _rms(_dot(xb, w_ref[...]), er[0][...])
            continue
        for c0 in range(0, n, col_chunk):
            c1 = min(n, c0 + col_chunk)
            y = _dot(xb, w_ref[:, c0:c1])
            if e == "none":
                o_ref[:, c0:c1] = y
            else:
                width = LANES if e == "rope128" else DIFF_DIM
                cos, sin = er[0][...], er[1][...]
                for t0 in range(0, c1 - c0, LANES):
                    o_ref[:, c0 + t0:c0 + t0 + LANES] = _rope_tile(y[:, t0:t0 + LANES], cos, sin, width)


def _proj(x, gain, weights, epis, extras, out_widths, tm):
    m, k = x.shape
    assert m % tm == 0
    row = lambda i: (i, 0)
    fixed = lambda i: (0, 0)
    args, specs = [x], [pl.BlockSpec((tm, k), row)]
    if gain is not None:
        args.append(gain.reshape(1, k))
        specs.append(pl.BlockSpec((1, k), fixed))
    for w in weights:
        args.append(w)
        specs.append(pl.BlockSpec(w.shape, fixed, pipeline_mode=pl.Buffered(1)))
    for e, ex in zip(epis, extras):
        if e in ("rope128", "rope64"):
            for t in ex:
                args.append(t)
                specs.append(pl.BlockSpec((tm, LANES), row))
        elif e == "rms":
            args.append(ex[0].reshape(1, -1))
            specs.append(pl.BlockSpec((1, ex[0].size), fixed))
    out_shape = [jax.ShapeDtypeStruct((m, n), F32) for n in out_widths]
    out_specs = [pl.BlockSpec((tm, n), row) for n in out_widths]
    kern = functools.partial(_proj_kernel, n_w=len(weights), norm_in=gain is not None, epis=tuple(epis),
                             col_chunk=512)
    return pl.pallas_call(kern, grid=(m // tm,), in_specs=specs, out_specs=out_specs, out_shape=out_shape,
                          compiler_params=_cparams(("parallel",)))(*args)


def _outproj_kernel(*refs, n_a):
    a_refs = refs[:n_a]
    w_refs = refs[n_a:2 * n_a]
    g_ref, x_ref, o_ref = refs[2 * n_a:]
    y = _dot(a_refs[0][...].astype(BF16), w_refs[0][...])
    for a_ref, w_ref in zip(a_refs[1:], w_refs[1:]):
        y = y + _dot(a_ref[...].astype(BF16), w_ref[...])
    o_ref[...] = x_ref[...] + _rms(y, g_ref[...])


def _outproj(acts, weights, gain, resid, tm):
    m, d = resid.shape
    row = lambda i: (i, 0)
    fixed = lambda i: (0, 0)
    specs = [pl.BlockSpec((tm, a.shape[1]), row) for a in acts]
    specs += [pl.BlockSpec(w.shape, fixed, pipeline_mode=pl.Buffered(1)) for w in weights]
    specs += [pl.BlockSpec((1, d), fixed), pl.BlockSpec((tm, d), row)]
    return pl.pallas_call(functools.partial(_outproj_kernel, n_a=len(acts)), grid=(m // tm,), in_specs=specs,
                          out_specs=pl.BlockSpec((tm, d), row), out_shape=jax.ShapeDtypeStruct((m, d), F32),
                          compiler_params=_cparams(("parallel",)))(*acts, *weights, gain.reshape(1, d), resid)


def _ffn_kernel(x_ref, g_in_ref, wg_ref, wu_ref, cw_ref, cb_ref, wd_ref, g_out_ref, past_ref,
                y_ref, tail_ref, h_s, acc_s, gbuf_s, carry_s, *, blocks_per_seq):
    i, f = pl.program_id(0), pl.program_id(1)
    n_seq, t_blk, tf = gbuf_s.shape[0], gbuf_s.shape[1] - SUBLANES, gbuf_s.shape[2]
    halo = CONV_W - 1

    @pl.when(f == 0)
    def _():
        h_s[...] = _rms(x_ref[...], g_in_ref[...]).astype(BF16)
        acc_s[...] = jnp.zeros_like(acc_s)
        if blocks_per_seq > 1:
            @pl.when(i == 0)
            def _():
                carry_s[...] = jnp.zeros_like(carry_s)

    h = h_s[...]
    g3 = _dot(h, wg_ref[...]).reshape(n_seq, t_blk, tf)
    u = _dot(h, wu_ref[...])

    if blocks_per_seq == 1:
        gbuf_s[:, 0:SUBLANES, :] = past_ref[...]
    else:
        gbuf_s[:, 0:SUBLANES, :] = jnp.where(i % blocks_per_seq == 0, past_ref[...], carry_s[f])
        carry_s[f] = g3[:, t_blk - SUBLANES:, :]
    gbuf_s[:, SUBLANES:, :] = g3
    tail_ref[...] = g3[:, t_blk - halo:, :]

    gc = cb_ref[...].reshape(1, 1, tf)
    for j in range(CONV_W):
        gc = gc + cw_ref[j:j + 1, :].reshape(1, 1, tf) * gbuf_s[:, SUBLANES - halo + j:SUBLANES - halo + j + t_blk, :]
    act = jax.nn.gelu(gc.reshape(n_seq * t_blk, tf), approximate=True) * u
    acc_s[...] += _dot(act.astype(BF16), wd_ref[...])

    @pl.when(f == pl.num_programs(1) - 1)
    def _():
        y_ref[...] = x_ref[...] + _rms(acc_s[...], g_out_ref[...])


def _ffn(x, past8, g_in, wg, wu, cw, cb, wd, g_out, seq_len, tm, tf):
    m, d = x.shape
    n_seq = m // seq_len
    if seq_len >= tm:
        blocks_per_seq, seq_per_blk, t_blk = seq_len // tm, 1, tm
    else:
        blocks_per_seq, seq_per_blk, t_blk = 1, tm // seq_len, seq_len
    n_f = D_FF // tf
    halo = CONV_W - 1
    seq_idx = lambda i, f: (i // blocks_per_seq, 0, f)
    in_specs = [
        pl.BlockSpec((tm, d), lambda i, f: (i, 0)),
        pl.BlockSpec((1, d), lambda i, f: (0, 0)),
        pl.BlockSpec((d, tf), lambda i, f: (0, f)),
        pl.BlockSpec((d, tf), lambda i, f: (0, f)),
        pl.BlockSpec((CONV_W, tf), lambda i, f: (0, f)),
        pl.BlockSpec((1, tf), lambda i, f: (0, f)),
        pl.BlockSpec((tf, d), lambda i, f: (f, 0)),
        pl.BlockSpec((1, d), lambda i, f: (0, 0)),
        pl.BlockSpec((seq_per_blk, SUBLANES, tf), seq_idx),
    ]
    out_specs = [pl.BlockSpec((tm, d), lambda i, f: (i, 0)),
                 pl.BlockSpec((seq_per_blk, halo, tf), lambda i, f: (i, 0, f))]
    out_shape = [jax.ShapeDtypeStruct((m, d), F32),
                 jax.ShapeDtypeStruct((m // tm * seq_per_blk, halo, D_FF), F32)]
    scratch = [pltpu.VMEM((tm, d), BF16), pltpu.VMEM((tm, d), F32),
               pltpu.VMEM((seq_per_blk, t_blk + SUBLANES, tf), F32),
               pltpu.VMEM((n_f, seq_per_blk, SUBLANES, tf), F32)]
    kern = functools.partial(_ffn_kernel, blocks_per_seq=blocks_per_seq)
    y, tails = pl.pallas_call(kern, grid=(m // tm, n_f), in_specs=in_specs, out_specs=out_specs,
                              out_shape=out_shape, scratch_shapes=scratch,
                              compiler_params=_cparams(("arbitrary", "arbitrary")))(
        x, g_in.reshape(1, d), wg, wu, cw, cb.reshape(1, D_FF), wd, g_out.reshape(1, d), past8)
    return y, tails[blocks_per_seq - 1::blocks_per_seq]


def _weighted_values(p, values):
    if not isinstance(values, (list, tuple)):
        return _dot(p, values)
    r = p.shape[0] // len(values)
    return jnp.concatenate([_dot(p[i * r:(i + 1) * r], v) for i, v in enumerate(values)], axis=0)


def _softmax_init(s, values, m_s, l_s, acc_s):
    m = jnp.max(s, axis=-1, keepdims=True)
    p = jnp.exp(s - m)
    m_s[...] = m
    l_s[...] = jnp.sum(p, axis=-1, keepdims=True)
    acc_s[...] = _weighted_values(p.astype(BF16), values)


def _softmax_update(s, values, m_s, l_s, acc_s):
    m_old = m_s[...]
    m_new = jnp.maximum(m_old, jnp.max(s, axis=-1, keepdims=True))
    alpha = jnp.exp(m_old - m_new)
    p = jnp.exp(s - m_new)
    m_s[...] = m_new
    l_s[...] = alpha * l_s[...] + jnp.sum(p, axis=-1, keepdims=True)
    acc_s[...] = alpha * acc_s[...] + _weighted_values(p.astype(BF16), values)


def _topk_mask(gate, valid, k, axis=1):
    lane = lax.broadcasted_iota(jnp.int32, gate.shape, axis)
    big = jnp.int32(gate.shape[axis])
    sel = jnp.zeros(gate.shape, jnp.bool_)
    rem = valid
    for _ in range(k):
        gm = jnp.where(rem, gate, NEG_INF)
        top = jnp.max(gm, axis=axis, keepdims=True)
        idx = jnp.min(jnp.where(rem & (gm == top), lane, big), axis=axis, keepdims=True)
        pick = lane == idx
        sel = sel | pick
        rem = rem & jnp.logical_not(pick)
    return sel


def _causal_tile(rows, cols, period):
    r = lax.broadcasted_iota(jnp.int32, (rows, cols), 0) % period
    c = lax.broadcasted_iota(jnp.int32, (rows, cols), 1)
    return c <= r


def _diff_lambda(lam_ref, lam_init):
    lv = lam_ref[...]
    d1 = jnp.sum(lv[0:1] * lv[1:2], axis=-1, keepdims=True)
    d2 = jnp.sum(lv[2:3] * lv[3:4], axis=-1, keepdims=True)
    return jnp.exp(d1) - jnp.exp(d2) + lam_init


def _split_maps(q2):
    lane = lax.broadcasted_iota(jnp.int32, q2.shape, 1)
    lo = jnp.where(lane < DIFF_DIM, q2, 0.0)
    hi = jnp.where(lane >= DIFF_DIM, q2, 0.0)
    return jnp.concatenate([lo, hi], axis=0)


TQ = MOBA_BLOCK
DIFF_HEAD_GROUP = 4
MLA_HEAD_GROUP = 8
LOG2_E = math.log2(math.e)


def _lane_halves_max(s):
    out = s[:, :LANES]
    for c0 in range(LANES, s.shape[1], LANES):
        out = jnp.maximum(out, s[:, c0:c0 + LANES])
    return out


def _lane_halves_sum(p):
    out = p[:, :LANES]
    for c0 in range(LANES, p.shape[1], LANES):
        out = out + p[:, c0:c0 + LANES]
    return out


def _two_pass_attend(qk, values, own_mask, past_bias, qi, scale, mx_s, l_s, acc_s):
    def raw_scores(j, k0):
        s = qk(k0)
        if past_bias is not None:
            s = s + jnp.concatenate([past_bias(j)] * (TQ // LANES), axis=1)
        return s

    own0 = pl.multiple_of(qi * TQ, TQ)
    s_own = jnp.where(own_mask, qk(own0), NEG_INF)
    mx_s[...] = _lane_halves_max(s_own)

    def pass1(j, carry):
        mx_s[...] = jnp.maximum(mx_s[...], _lane_halves_max(raw_scores(j, pl.multiple_of(j * TQ, TQ))))
        return carry

    lax.fori_loop(0, qi, pass1, 0)
    mx_s[...] = jnp.broadcast_to(jnp.max(mx_s[...], axis=-1, keepdims=True), mx_s.shape)

    def probs(s):
        m = jnp.concatenate([mx_s[...]] * (TQ // LANES), axis=1)
        return jnp.exp2((s - m) * (scale * LOG2_E))

    p = probs(s_own)
    l_s[...] = _lane_halves_sum(p)
    acc_s[...] = _weighted_values(p.astype(BF16), values(own0))

    def pass2(j, carry):
        k0 = pl.multiple_of(j * TQ, TQ)
        p = probs(raw_scores(j, k0))
        l_s[...] += _lane_halves_sum(p)
        acc_s[...] += _weighted_values(p.astype(BF16), values(k0))
        return carry

    lax.fori_loop(0, qi, pass2, 0)
    return acc_s[...] / jnp.sum(l_s[...], axis=-1, keepdims=True)


def _moba_prefill_kernel(q_ref, k_ref, v_ref, o_ref, kb_s, vb_s, bias_s, mx_s, l_s, acc_s):
    qi = pl.program_id(2)
    t = k_ref.shape[0]
    nb = t // MOBA_BLOCK
    n_sel = max(1, min(MOBA_TOPK, nb - 1))
    scale = HEAD_DIM ** -0.5
    g_per = MOBA_HEADS // MOBA_KV_HEADS

    @pl.when(qi == 0)
    def _():
        kb_s[...] = k_ref[...].astype(BF16)
        vb_s[...] = v_ref[...].astype(BF16)

    kmean = jnp.mean(k_ref[...].reshape(nb, MOBA_BLOCK, HEAD_DIM), axis=1)
    rows = g_per * TQ
    qf = _rows_by_head(q_ref, range(g_per), HEAD_DIM)
    qb = qf.astype(BF16)
    blk = lax.broadcasted_iota(jnp.int32, (nb, rows), 0)
    gate = _dot_nt(kmean, qf, lax.Precision.HIGHEST)
    sel = _topk_mask(gate, blk < qi, n_sel, axis=0).astype(F32)
    spread = (lax.broadcasted_iota(jnp.int32, (nb, nb * LANES), 1) // LANES
              == lax.broadcasted_iota(jnp.int32, (nb, nb * LANES), 0)).astype(F32)
    keep = lax.dot_general(sel, spread, (((0,), (0,)), ((), ())), preferred_element_type=F32)
    for j in range(nb):
        bias_s[j] = jnp.where(keep[:, j * LANES:(j + 1) * LANES] > 0.5, 0.0, NEG_INF)
    o = _two_pass_attend(lambda k0: _dot_nt(qb, kb_s[pl.ds(k0, TQ), :]), lambda k0: vb_s[pl.ds(k0, TQ), :],
                         _causal_tile(rows, TQ, TQ), lambda j: bias_s[j], qi, scale, mx_s, l_s, acc_s)
    for g in range(g_per):
        o_ref[:, g * HEAD_DIM:(g + 1) * HEAD_DIM] = o[g * TQ:(g + 1) * TQ]


def _moba_prefill(mq, mk, mv, n_batch, t):
    m = mq.shape[0]
    nq = t // TQ
    g_w = MOBA_HEADS // MOBA_KV_HEADS * HEAD_DIM
    q_spec = pl.BlockSpec((TQ, g_w), lambda b, k, i: (b * nq + i, k))
    kv_spec = pl.BlockSpec((t, HEAD_DIM), lambda b, k, i: (b, k))
    rows = MOBA_HEADS // MOBA_KV_HEADS * TQ
    scratch = [pltpu.VMEM((t, HEAD_DIM), BF16), pltpu.VMEM((t, HEAD_DIM), BF16),
               pltpu.VMEM((t // MOBA_BLOCK, rows, LANES), F32),
               pltpu.VMEM((rows, LANES), F32), pltpu.VMEM((rows, LANES), F32), pltpu.VMEM((rows, HEAD_DIM), F32)]
    return pl.pallas_call(_moba_prefill_kernel, grid=(n_batch, MOBA_KV_HEADS, nq),
                          in_specs=[q_spec, kv_spec, kv_spec], out_specs=q_spec,
                          out_shape=jax.ShapeDtypeStruct((m, MOBA_HEADS * HEAD_DIM), F32), scratch_shapes=scratch,
                          compiler_params=_cparams(("parallel", "parallel", "arbitrary")))(mq, mk, mv)


def _diff_finish(o2, lam, sub_g, lam_init, rows):
    o = o2[:rows] - lam * o2[rows:]
    return _rms(o, sub_g) * (1.0 - lam_init)


def _diff_prefill_kernel(q_ref, k_ref, v_ref, lam_ref, subg_ref, o_ref, kb_s, vb_s, mx_s, l_s, acc_s, *, lam_init):
    qi = pl.program_id(2)
    scale = DIFF_DIM ** -0.5
    g_per = DIFF_HEADS // DIFF_KV_HEADS
    width = 2 * DIFF_DIM

    @pl.when(qi == 0)
    def _():
        kb_s[...] = k_ref[...].astype(BF16)
        vb_s[...] = v_ref[...].astype(BF16)

    lam = _diff_lambda(lam_ref, lam_init)
    grp = mx_s.shape[0] // (2 * TQ)
    tril = _causal_tile(grp * 2 * TQ, TQ, TQ)
    for g0 in range(0, g_per, grp):
        qb = jnp.concatenate([_split_maps(q_ref[:, g * width:(g + 1) * width]) for g in range(g0, g0 + grp)],
                             axis=0).astype(BF16)
        o2 = _two_pass_attend(lambda k0: _dot_nt(qb, kb_s[pl.ds(k0, TQ), :]), lambda k0: vb_s[pl.ds(k0, TQ), :],
                              tril, None, qi, scale, mx_s, l_s, acc_s)
        for i in range(grp):
            g = g0 + i
            o_ref[:, g * width:(g + 1) * width] = _diff_finish(
                o2[i * 2 * TQ:(i + 1) * 2 * TQ], lam, subg_ref[...], lam_init, TQ)


def _diff_prefill(dq, dk, dv, lam_vec, sub_g, lam_init, n_batch, t):
    m = dq.shape[0]
    nq = t // TQ
    width = 2 * DIFF_DIM
    g_w = DIFF_HEADS // DIFF_KV_HEADS * width
    q_spec = pl.BlockSpec((TQ, g_w), lambda b, k, i: (b * nq + i, k))
    kv_spec = pl.BlockSpec((t, width), lambda b, k, i: (b, k))
    fixed = lambda b, k, i: (0, 0)
    rows = DIFF_HEAD_GROUP * 2 * TQ
    scratch = [pltpu.VMEM((t, width), BF16), pltpu.VMEM((t, width), BF16),
               pltpu.VMEM((rows, LANES), F32), pltpu.VMEM((rows, LANES), F32), pltpu.VMEM((rows, width), F32)]
    kern = functools.partial(_diff_prefill_kernel, lam_init=lam_init)
    return pl.pallas_call(kern, grid=(n_batch, DIFF_KV_HEADS, nq),
                          in_specs=[q_spec, kv_spec, kv_spec, pl.BlockSpec((4, DIFF_DIM), fixed),
                                    pl.BlockSpec((1, width), fixed)],
                          out_specs=q_spec, out_shape=jax.ShapeDtypeStruct((m, DIFF_HEADS * width), F32),
                          scratch_shapes=scratch,
                          compiler_params=_cparams(("parallel", "parallel", "arbitrary")))(
        dq, dk, dv, lam_vec, sub_g.reshape(1, width))


def _mla_prefill_kernel(qn_ref, qr_ref, ckv_ref, kr2_ref, wuk_ref, wuv_ref, o_ref, kx_s, vx_s, mx_s, l_s, acc_s):
    qi = pl.program_id(2)
    grp = kx_s.shape[0]
    scale = (NOPE_DIM + ROPE_DIM) ** -0.5

    @pl.when(qi == 0)
    def _():
        cb = ckv_ref[...].astype(BF16)
        kr = kr2_ref[...].astype(BF16)
        for i in range(grp):
            kx_s[i, :, :NOPE_DIM] = _dot_nt(cb, wuk_ref[i]).astype(BF16)
            kx_s[i, :, NOPE_DIM:] = kr
            vx_s[i] = _dot(cb, wuv_ref[i]).astype(BF16)

    lane = lax.broadcasted_iota(jnp.int32, (TQ, LANES), 1)
    q_heads = []
    for i in range(grp):
        pair = qr_ref[:, (i // 2) * LANES:(i // 2 + 1) * LANES]
        in_head = (lane < ROPE_DIM) if i % 2 == 0 else (lane >= ROPE_DIM)
        q_r = jnp.where(in_head, pair, 0.0)
        q_heads.append(jnp.concatenate([qn_ref[:, i * NOPE_DIM:(i + 1) * NOPE_DIM], q_r], axis=1).astype(BF16))

    def qk(k0):
        return jnp.concatenate([_dot_nt(q_heads[i], kx_s[i, pl.ds(k0, TQ), :]) for i in range(grp)], axis=0)

    o = _two_pass_attend(qk, lambda k0: [vx_s[i, pl.ds(k0, TQ), :] for i in range(grp)],
                         _causal_tile(grp * TQ, TQ, TQ), None, qi, scale, mx_s, l_s, acc_s)
    for i in range(grp):
        o_ref[:, i * V_DIM:(i + 1) * V_DIM] = o[i * TQ:(i + 1) * TQ]


def _mla_prefill(q_nope, q_rope, ckv, kr2, wuk_t, wuv_t, n_batch, t):
    m = q_nope.shape[0]
    nq = t // TQ
    grp = MLA_HEAD_GROUP
    row = lambda b, g, i: (b * nq + i, g)
    seq = lambda b, g, i: (b, 0)
    head = lambda b, g, i: (g, 0, 0)
    scratch = [pltpu.VMEM((grp, t, NOPE_DIM + LANES), BF16), pltpu.VMEM((grp, t, V_DIM), BF16),
               pltpu.VMEM((grp * TQ, LANES), F32), pltpu.VMEM((grp * TQ, LANES), F32),
               pltpu.VMEM((grp * TQ, V_DIM), F32)]
    in_specs = [pl.BlockSpec((TQ, grp * NOPE_DIM), row), pl.BlockSpec((TQ, grp * ROPE_DIM), row),
                pl.BlockSpec((t, KV_LORA), seq), pl.BlockSpec((t, LANES), seq),
                pl.BlockSpec((grp,) + wuk_t.shape[1:], head), pl.BlockSpec((grp,) + wuv_t.shape[1:], head)]
    return pl.pallas_call(_mla_prefill_kernel, grid=(n_batch, MLA_HEADS // grp, nq), in_specs=in_specs,
                          out_specs=pl.BlockSpec((TQ, grp * V_DIM), row),
                          out_shape=jax.ShapeDtypeStruct((m, MLA_HEADS * V_DIM), F32), scratch_shapes=scratch,
                          compiler_params=_cparams(("parallel", "parallel", "arbitrary")))(
        q_nope, q_rope, ckv, kr2, wuk_t, wuv_t)


PAGES_PER_STEP = 16
N_PAGES = PAST_LEN // PAGE_SIZE


def _page_copy(pt_ref, pool, buf, sem, step, slot, u, pages_per_step):
    return pltpu.make_async_copy(pool.at[pt_ref[step * pages_per_step + u]], buf.at[slot, u], sem.at[slot])


PAGE_BUFFERS = 3


def _fetch_pages(pt_ref, pools, bufs, sems, pages_per_step):
    n_chunk = pl.num_programs(1)
    n_step = pl.num_programs(0) * n_chunk
    g = pl.program_id(0) * n_chunk + pl.program_id(1)
    ahead = PAGE_BUFFERS - 1

    def copies(step):
        slot = step % PAGE_BUFFERS
        return [_page_copy(pt_ref, pool, buf, sems.at[i], step, slot, u, pages_per_step)
                for u in range(pages_per_step) for i, (pool, buf) in enumerate(zip(pools, bufs))]

    for first in range(ahead):
        @pl.when(jnp.logical_and(g == 0, first < n_step))
        def _():
            for cp in copies(first):
                cp.start()

    @pl.when(g + ahead < n_step)
    def _():
        for cp in copies(g + ahead):
            cp.start()

    for cp in copies(g):
        cp.wait()
    slot = g % PAGE_BUFFERS
    return [[buf.at[slot, u] for u in range(pages_per_step)] for buf in bufs]


def _paged_operands(pools, pages_per_step):
    specs = [pl.BlockSpec(memory_space=pl.ANY) for _ in pools]
    bufs = [pltpu.VMEM((PAGE_BUFFERS, pages_per_step) + pool.shape[1:], pool.dtype) for pool in pools]
    return specs, bufs, pltpu.SemaphoreType.DMA((len(pools), PAGE_BUFFERS))


def _head_rows(refs, kvh, n_kvh):
    return jnp.concatenate([r[pl.ds(kvh, PAGE_SIZE, stride=n_kvh), :] for r in refs], axis=0)


def _rows_by_head(ref, heads, width):
    return jnp.concatenate([ref[:, h * width:(h + 1) * width] for h in heads], axis=0)


def _moba_decode_kernel(pt_ref, q_ref, kn_ref, vn_ref, k_pool, v_pool, o_ref, m_s, l_s, o_s, km_s,
                        k_buf, v_buf, sems, *, pages_per_step, t_new):
    p = pages_per_step
    k_refs, v_refs = _fetch_pages(pt_ref, (k_pool, v_pool), (k_buf, v_buf), sems, p)
    c = pl.program_id(1)
    scale = HEAD_DIM ** -0.5
    g_per = MOBA_HEADS // MOBA_KV_HEADS
    rows = g_per * t_new
    pages_per_blk = MOBA_BLOCK // PAGE_SIZE
    blk_per_step = p // pages_per_blk
    n_past_blk = PAST_LEN // MOBA_BLOCK
    n_sel = max(1, min(MOBA_TOPK, n_past_blk))

    lane = lax.broadcasted_iota(jnp.int32, (rows, LANES), 1)

    @pl.when(c == 0)
    def _():
        m_s[...] = jnp.zeros_like(m_s)
        l_s[...] = jnp.zeros_like(l_s)

    for kvh in range(MOBA_KV_HEADS):
        q_rows = _rows_by_head(q_ref, range(kvh * g_per, (kvh + 1) * g_per), HEAD_DIM)
        qb = q_rows.astype(BF16)
        kf = _head_rows(k_refs, kvh, MOBA_KV_HEADS)
        vb = _head_rows(v_refs, kvh, MOBA_KV_HEADS).astype(BF16)
        s_all = _dot_nt(qb, kf.astype(BF16)) * scale
        m_mat, l_mat = m_s[kvh], l_s[kvh]
        for b in range(blk_per_step):
            k0, k1 = b * MOBA_BLOCK, (b + 1) * MOBA_BLOCK
            s = s_all[:, k0:k1]
            m = jnp.max(s, axis=-1, keepdims=True)
            e = jnp.exp(s - m)
            blk = c * blk_per_step + b
            m_mat = jnp.where(lane == blk, m, m_mat)
            l_mat = jnp.where(lane == blk, jnp.sum(e, axis=-1, keepdims=True), l_mat)
            o_s[kvh, blk] = _dot(e.astype(BF16), vb[k0:k1])
            km_s[kvh, pl.ds(blk, 1), :] = jnp.mean(kf[k0:k1], axis=0, keepdims=True)
        m_s[kvh] = m_mat
        l_s[kvh] = l_mat

    @pl.when(c == pl.num_programs(1) - 1)
    def _():
        causal = _causal_tile(MOBA_KV_HEADS * rows, t_new, t_new)
        q_k = [_rows_by_head(q_ref, range(k * g_per, (k + 1) * g_per), HEAD_DIM) for k in range(MOBA_KV_HEADS)]
        gate = jnp.concatenate([_dot_nt(q_k[k], km_s[k], lax.Precision.HIGHEST) for k in range(MOBA_KV_HEADS)],
                               axis=0)
        sel = _topk_mask(gate, jnp.ones(gate.shape, jnp.bool_), n_sel)
        s_own = jnp.concatenate(
            [_dot_nt(q_k[k].astype(BF16), kn_ref[:, k * HEAD_DIM:(k + 1) * HEAD_DIM].astype(BF16))
             for k in range(MOBA_KV_HEADS)], axis=0) * scale
        s_own = jnp.where(causal, s_own, NEG_INF)
        m_blk = jnp.concatenate([m_s[k] for k in range(MOBA_KV_HEADS)], axis=0)[:, :n_past_blk]
        l_blk = jnp.concatenate([l_s[k] for k in range(MOBA_KV_HEADS)], axis=0)[:, :n_past_blk]
        m_fin = jnp.maximum(jnp.max(s_own, axis=-1, keepdims=True),
                            jnp.max(jnp.where(sel, m_blk, NEG_INF), axis=-1, keepdims=True))
        e_own = jnp.exp(s_own - m_fin)
        w = jnp.exp(jnp.where(sel, m_blk - m_fin, NEG_INF))
        l_fin = jnp.sum(e_own, axis=-1, keepdims=True) + jnp.sum(w * l_blk, axis=-1, keepdims=True)
        spread = (lax.broadcasted_iota(jnp.int32, (n_past_blk, n_past_blk * LANES), 1) // LANES
                  == lax.broadcasted_iota(jnp.int32, (n_past_blk, n_past_blk * LANES), 0)).astype(BF16)
        w_head = w.astype(BF16)
        w_lanes = _dot(w_head, spread) + _dot((w - w_head.astype(F32)).astype(BF16), spread)
        for k in range(MOBA_KV_HEADS):
            r0, r1 = k * rows, (k + 1) * rows
            parts = [_dot(e_own[r0:r1].astype(BF16), vn_ref[:, k * HEAD_DIM:(k + 1) * HEAD_DIM].astype(BF16)),
                     jnp.zeros((rows, HEAD_DIM), F32)]
            for b in range(n_past_blk):
                parts[b % 2] = parts[b % 2] + w_lanes[r0:r1, b * LANES:(b + 1) * LANES] * o_s[k, b]
            o = (parts[0] + parts[1]) / l_fin[r0:r1]
            for i in range(g_per):
                h = k * g_per + i
                o_ref[:, h * HEAD_DIM:(h + 1) * HEAD_DIM] = o[i * t_new:(i + 1) * t_new]


def _moba_decode(mq, mk, mv, pool_k, pool_v, page_table, t_new):
    n_seq = mq.shape[0] // t_new
    p = PAGES_PER_STEP
    kv_w = MOBA_KV_HEADS * HEAD_DIM
    g_per = MOBA_HEADS // MOBA_KV_HEADS
    rows = g_per * t_new
    n_blk = PAST_LEN // MOBA_BLOCK
    seq = lambda s, c, pt: (s, 0)
    pool_specs, bufs, sems = _paged_operands((pool_k, pool_v), p)
    in_specs = [pl.BlockSpec((t_new, MOBA_HEADS * HEAD_DIM), seq), pl.BlockSpec((t_new, kv_w), seq),
                pl.BlockSpec((t_new, kv_w), seq)] + pool_specs
    assert n_blk <= LANES
    scratch = [pltpu.VMEM((MOBA_KV_HEADS, rows, LANES), F32), pltpu.VMEM((MOBA_KV_HEADS, rows, LANES), F32),
               pltpu.VMEM((MOBA_KV_HEADS, n_blk, rows, HEAD_DIM), F32), pltpu.VMEM((MOBA_KV_HEADS, n_blk, HEAD_DIM), F32)]
    grid_spec = pltpu.PrefetchScalarGridSpec(
        num_scalar_prefetch=1, grid=(n_seq, N_PAGES // p), in_specs=in_specs,
        out_specs=pl.BlockSpec((t_new, MOBA_HEADS * HEAD_DIM), seq), scratch_shapes=scratch + bufs + [sems])
    kern = functools.partial(_moba_decode_kernel, pages_per_step=p, t_new=t_new)
    return pl.pallas_call(kern, grid_spec=grid_spec, out_shape=jax.ShapeDtypeStruct(mq.shape, F32),
                          compiler_params=_cparams(("arbitrary", "arbitrary")))(
        page_table.reshape(-1), mq, mk, mv, pool_k, pool_v)


def _diff_decode_kernel(pt_ref, q_ref, kn_ref, vn_ref, lam_ref, subg_ref, k_pool, v_pool, o_ref, m_s, l_s, acc_s,
                        k_buf, v_buf, sems, *, pages_per_step, t_new, lam_init):
    p = pages_per_step
    k_refs, v_refs = _fetch_pages(pt_ref, (k_pool, v_pool), (k_buf, v_buf), sems, p)
    c = pl.program_id(1)
    scale = DIFF_DIM ** -0.5
    g_per = DIFF_HEADS // DIFF_KV_HEADS
    width = 2 * DIFF_DIM
    rows = g_per * t_new
    kvhs = range(DIFF_KV_HEADS)
    qb = [_split_maps(_rows_by_head(q_ref, range(k * g_per, (k + 1) * g_per), width)).astype(BF16) for k in kvhs]

    @pl.when(c == 0)
    def _():
        causal = _causal_tile(DIFF_KV_HEADS * 2 * rows, t_new, t_new)
        s_own = jnp.concatenate([_dot_nt(qb[k], kn_ref[:, k * width:(k + 1) * width].astype(BF16)) for k in kvhs],
                                axis=0) * scale
        _softmax_init(jnp.where(causal, s_own, NEG_INF),
                      [vn_ref[:, k * width:(k + 1) * width].astype(BF16) for k in kvhs], m_s, l_s, acc_s)

    s = jnp.concatenate(
        [_dot(qb[k], jnp.concatenate([r[k * width:(k + 1) * width, :] for r in k_refs], axis=1).astype(BF16))
         for k in kvhs], axis=0) * scale
    _softmax_update(s, [_head_rows(v_refs, k, DIFF_KV_HEADS).astype(BF16) for k in kvhs], m_s, l_s, acc_s)

    @pl.when(c == pl.num_programs(1) - 1)
    def _():
        lam = _diff_lambda(lam_ref, lam_init)
        o2 = acc_s[...] / l_s[...]
        for k in kvhs:
            o = _diff_finish(o2[k * 2 * rows:(k + 1) * 2 * rows], lam, subg_ref[...], lam_init, rows)
            for i in range(g_per):
                h = k * g_per + i
                o_ref[:, h * width:(h + 1) * width] = o[i * t_new:(i + 1) * t_new]


def _diff_decode(dq, dk, dv, pool_k, pool_v, page_table, lam_vec, sub_g, lam_init, t_new):
    n_seq = dq.shape[0] // t_new
    p = PAGES_PER_STEP
    width = 2 * DIFF_DIM
    kv_w = DIFF_KV_HEADS * width
    rows2 = 2 * (DIFF_HEADS // DIFF_KV_HEADS) * t_new
    seq = lambda s, c, pt: (s, 0)
    fixed = lambda s, c, pt: (0, 0)
    in_specs = [pl.BlockSpec((t_new, DIFF_HEADS * width), seq), pl.BlockSpec((t_new, kv_w), seq),
                pl.BlockSpec((t_new, kv_w), seq), pl.BlockSpec((4, DIFF_DIM), fixed),
                pl.BlockSpec((1, width), fixed)]
    pool_specs, bufs, sems = _paged_operands((pool_k, pool_v), p)
    scratch = [pltpu.VMEM((DIFF_KV_HEADS * rows2, 1), F32), pltpu.VMEM((DIFF_KV_HEADS * rows2, 1), F32),
               pltpu.VMEM((DIFF_KV_HEADS * rows2, width), F32)]
    grid_spec = pltpu.PrefetchScalarGridSpec(
        num_scalar_prefetch=1, grid=(n_seq, N_PAGES // p), in_specs=in_specs + pool_specs,
        out_specs=pl.BlockSpec((t_new, DIFF_HEADS * width), seq), scratch_shapes=scratch + bufs + [sems])
    kern = functools.partial(_diff_decode_kernel, pages_per_step=p, t_new=t_new, lam_init=lam_init)
    return pl.pallas_call(kern, grid_spec=grid_spec, out_shape=jax.ShapeDtypeStruct(dq.shape, F32),
                          compiler_params=_cparams(("arbitrary", "arbitrary")))(
        page_table.reshape(-1), dq, dk, dv, lam_vec, sub_g.reshape(1, width), pool_k, pool_v)


def _mla_decode_kernel(pt_ref, qn_ref, qr_ref, cn_ref, krn_ref, wuk_ref, wuv_ref, c_pool, r_pool, o_ref,
                       qlat_s, m_s, l_s, acc_s, c_buf, r_buf, sems, *, pages_per_step, t_new):
    p = pages_per_step
    c_refs, r_refs = _fetch_pages(pt_ref, (c_pool, r_pool), (c_buf, r_buf), sems, p)
    c = pl.program_id(1)
    scale = (NOPE_DIM + ROPE_DIM) ** -0.5
    rows = MLA_HEADS * t_new

    @pl.when(c == 0)
    def _():
        for h in range(MLA_HEADS):
            qlat_s[h * t_new:(h + 1) * t_new, :] = _dot(
                qn_ref[:, h * NOPE_DIM:(h + 1) * NOPE_DIM].astype(BF16), wuk_ref[h])
        q_lat = qlat_s[...].astype(BF16)
        causal = _causal_tile(rows, t_new, t_new)
        cn = cn_ref[...].astype(BF16)
        s_own = (_dot_nt(q_lat, cn) + _dot_nt(qr_ref[...].astype(BF16), krn_ref[...].astype(BF16))) * scale
        _softmax_init(jnp.where(causal, s_own, NEG_INF), cn, m_s, l_s, acc_s)

    q_lat = qlat_s[...].astype(BF16)
    ckv = jnp.concatenate([r[...] for r in c_refs], axis=0).astype(BF16)
    kr_t = jnp.concatenate([r[...] for r in r_refs], axis=1).astype(BF16)
    s = (_dot_nt(q_lat, ckv) + _dot(qr_ref[...].astype(BF16), kr_t)) * scale
    _softmax_update(s, ckv, m_s, l_s, acc_s)

    @pl.when(c == pl.num_programs(1) - 1)
    def _():
        o_lat = (acc_s[...] / l_s[...]).astype(BF16)
        for h in range(MLA_HEADS):
            o_ref[:, h * V_DIM:(h + 1) * V_DIM] = _dot(o_lat[h * t_new:(h + 1) * t_new], wuv_ref[h])


def _mla_decode(q_nope, q_rope_rows, ckv_new, kr_new, pool_c, pool_r, page_table, wuk_t, wuv_t, t_new):
    n_seq = q_nope.shape[0] // t_new
    p = PAGES_PER_STEP
    rows = MLA_HEADS * t_new
    seq = lambda s, c, pt: (s, 0)
    fixed3 = lambda s, c, pt: (0, 0, 0)
    in_specs = [pl.BlockSpec((t_new, MLA_HEADS * NOPE_DIM), seq),
                pl.BlockSpec((None, rows, ROPE_DIM), lambda s, c, pt: (s, 0, 0)),
                pl.BlockSpec((t_new, KV_LORA), seq), pl.BlockSpec((t_new, ROPE_DIM), seq),
                pl.BlockSpec(wuk_t.shape, fixed3, pipeline_mode=pl.Buffered(1)),
                pl.BlockSpec(wuv_t.shape, fixed3, pipeline_mode=pl.Buffered(1))]
    pool_specs, bufs, sems = _paged_operands((pool_c, pool_r), p)
    scratch = [pltpu.VMEM((rows, KV_LORA), F32), pltpu.VMEM((rows, 1), F32), pltpu.VMEM((rows, 1), F32),
               pltpu.VMEM((rows, KV_LORA), F32)]
    grid_spec = pltpu.PrefetchScalarGridSpec(
        num_scalar_prefetch=1, grid=(n_seq, N_PAGES // p), in_specs=in_specs + pool_specs,
        out_specs=pl.BlockSpec((t_new, MLA_HEADS * V_DIM), seq), scratch_shapes=scratch + bufs + [sems])
    kern = functools.partial(_mla_decode_kernel, pages_per_step=p, t_new=t_new)
    return pl.pallas_call(kern, grid_spec=grid_spec,
                          out_shape=jax.ShapeDtypeStruct((n_seq * t_new, MLA_HEADS * V_DIM), F32),
                          compiler_params=_cparams(("arbitrary", "arbitrary")))(
        page_table.reshape(-1), q_nope, q_rope_rows, ckv_new, kr_new, wuk_t, wuv_t, pool_c, pool_r)


ROW_TILE = 512
FF_TILE = 512


def _rope_tables(pos, width):
    inv = 1.0 / (ROPE_THETA ** (jnp.arange(0, width, 2, dtype=F32) / width))
    ang = pos.astype(F32)[:, None] * inv
    cos, sin = jnp.cos(ang), jnp.sin(ang)
    reps = LANES // width
    return jnp.tile(jnp.concatenate([cos, cos], axis=1), (1, reps)), jnp.tile(jnp.concatenate([-sin, sin], axis=1), (1, reps))


def _prep_weights(w_in_ab, w_out_ab, mla_w_dq, mla_w_uq, mla_w_dkv, mla_w_uk, mla_w_uv, mla_w_o,
                  ffn_w_gate, ffn_w_up, ffn_w_down):
    bf = lambda w: w.astype(BF16)
    cols = (MOBA_HEADS * HEAD_DIM, MOBA_KV_HEADS * HEAD_DIM, MOBA_KV_HEADS * HEAD_DIM,
            DIFF_HEADS * 2 * DIFF_DIM, DIFF_KV_HEADS * 2 * DIFF_DIM, DIFF_KV_HEADS * 2 * DIFF_DIM)
    offs = [sum(cols[:i]) for i in range(len(cols) + 1)]
    w_in = bf(w_in_ab[0])
    w_out = bf(w_out_ab[0])
    uq = bf(mla_w_uq[0]).reshape(-1, MLA_HEADS, NOPE_DIM + ROPE_DIM)
    dkv = bf(mla_w_dkv[0])
    return dict(
        ab_in=[w_in[:, offs[i]:offs[i + 1]] for i in range(len(cols))],
        ab_out=[w_out[:MOBA_HEADS * HEAD_DIM], w_out[MOBA_HEADS * HEAD_DIM:]],
        dq=bf(mla_w_dq[0]),
        uq_nope=uq[:, :, :NOPE_DIM].reshape(-1, MLA_HEADS * NOPE_DIM),
        uq_rope=uq[:, :, NOPE_DIM:].reshape(-1, MLA_HEADS * ROPE_DIM),
        dkv_c=dkv[:, :KV_LORA],
        dkv_r2=jnp.concatenate([dkv[:, KV_LORA:], dkv[:, KV_LORA:]], axis=1),
        uk_t=bf(mla_w_uk[0]).transpose(1, 2, 0),
        uv_t=bf(mla_w_uv[0]).transpose(1, 0, 2),
        o=bf(mla_w_o[0]),
        gate=[bf(w) for w in ffn_w_gate], up=[bf(w) for w in ffn_w_up], down=[bf(w) for w in ffn_w_down],
    )


def _trunk(x3, pos0, w, prm, past):
    n_seq, t, d = x3.shape
    x = x3.reshape(n_seq * t, d)
    pos = jnp.tile(pos0 + jnp.arange(t, dtype=jnp.int32), n_seq)
    rope128 = _rope_tables(pos, HEAD_DIM)
    rope64 = _rope_tables(pos, DIFF_DIM)
    gains = prm["norm_gains"]
    lam_init0 = 0.8 - 0.6 * math.exp(-0.3 * 0)
    halo = CONV_W - 1

    def ffn(xin, layer):
        if past is None:
            past8 = jnp.zeros((n_seq, SUBLANES, D_FF), F32)
        else:
            past8 = jnp.pad(past["conv"][layer], ((0, 0), (SUBLANES - halo, 0), (0, 0)))
        return _ffn(xin, past8, gains[layer, 2], w["gate"][layer], w["up"][layer], prm["ffn_conv_w"][layer],
                    prm["ffn_conv_b"][layer], w["down"][layer], gains[layer, 3], t, ROW_TILE, FF_TILE)

    mq, mk, mv, dq, dk, dv = _proj(
        x, gains[0, 0], w["ab_in"], ["rope128", "rope128", "none", "rope64", "rope64", "none"],
        [rope128, rope128, (), rope64, rope64, ()], [wi.shape[1] for wi in w["ab_in"]], ROW_TILE // 2)
    if past is None:
        o_m = _moba_prefill(mq, mk, mv, n_seq, t)
        o_d = _diff_prefill(dq, dk, dv, prm["diff_lambda"][0], prm["diff_subln"][0], lam_init0, n_seq, t)
    else:
        pt = past["page_table"]
        by_row = lambda pool: pool[0].reshape(pool.shape[1], -1, LANES)
        by_feature = lambda pool: pool[0].reshape(pool.shape[1], PAGE_SIZE, -1).transpose(0, 2, 1)
        o_m = _moba_decode(mq, mk, mv, by_row(past["moba"][0]), by_row(past["moba"][1]), pt, t)
        o_d = _diff_decode(dq, dk, dv, by_feature(past["diff"][0]), by_row(past["diff"][1]), pt,
                           prm["diff_lambda"][0], prm["diff_subln"][0], lam_init0, t)
    x = _outproj([o_m, o_d], w["ab_out"], gains[0, 1], x, ROW_TILE)
    x, conv0 = ffn(x, 0)

    cq, ckv, kr2 = _proj(x, gains[1, 0], [w["dq"], w["dkv_c"], w["dkv_r2"]], ["rms", "rms", "rope64"],
                         [(prm["mla_g_q"][0],), (prm["mla_g_kv"][0],), rope64], [Q_LORA, KV_LORA, LANES], ROW_TILE)
    q_nope, q_rope = _proj(cq, None, [w["uq_nope"], w["uq_rope"]], ["none", "rope64"], [(), rope64],
                           [MLA_HEADS * NOPE_DIM, MLA_HEADS * ROPE_DIM], ROW_TILE)
    kr = kr2[:, :ROPE_DIM]
    if past is None:
        o = _mla_prefill(q_nope, q_rope, ckv, kr2, w["uk_t"], w["uv_t"], n_seq, t)
    else:
        q_rope_rows = q_rope.reshape(n_seq, t, MLA_HEADS, ROPE_DIM).transpose(0, 2, 1, 3).reshape(
            n_seq, MLA_HEADS * t, ROPE_DIM)
        o = _mla_decode(q_nope, q_rope_rows, ckv, kr, past["mla"][0][0], past["mla"][1][0].transpose(0, 2, 1),
                        past["page_table"], w["uk_t"], w["uv_t"], t)
    x = _outproj([o], [w["o"]], gains[1, 1], x, ROW_TILE)
    x, conv1 = ffn(x, 1)

    new = (mk.reshape(1, n_seq, t, MOBA_KV_HEADS, HEAD_DIM), mv.reshape(1, n_seq, t, MOBA_KV_HEADS, HEAD_DIM),
           dk.reshape(1, n_seq, t, DIFF_KV_HEADS, 2, DIFF_DIM), dv.reshape(1, n_seq, t, DIFF_KV_HEADS, 2 * DIFF_DIM),
           ckv.reshape(1, n_seq, t, KV_LORA), kr.reshape(1, n_seq, t, ROPE_DIM), jnp.stack([conv0, conv1]))
    return x.reshape(n_seq, t, d), new


def kernel(x_prompt, x_sample, cache_moba_k, cache_moba_v, cache_diff_k, cache_diff_v, cache_mla_ckv,
           cache_mla_krope, state_ffn_conv, page_table, norm_gains, w_in_ab, w_out_ab, diff_lambda, diff_subln,
           mla_w_dq, mla_g_q, mla_w_uq, mla_w_dkv, mla_g_kv, mla_w_uk, mla_w_uv, mla_w_o,
           ffn_w_gate, ffn_w_up, ffn_conv_w, ffn_conv_b, ffn_w_down):
    w = _prep_weights(w_in_ab, w_out_ab, mla_w_dq, mla_w_uq, mla_w_dkv, mla_w_uk, mla_w_uv, mla_w_o,
                      ffn_w_gate, ffn_w_up, ffn_w_down)
    prm = dict(norm_gains=norm_gains, diff_lambda=diff_lambda, diff_subln=diff_subln, mla_g_q=mla_g_q,
               mla_g_kv=mla_g_kv, ffn_conv_w=ffn_conv_w, ffn_conv_b=ffn_conv_b)
    past = dict(moba=(cache_moba_k, cache_moba_v), diff=(cache_diff_k, cache_diff_v),
                mla=(cache_mla_ckv, cache_mla_krope), conv=state_ffn_conv, page_table=page_table)
    y_prompt, new_p = _trunk(x_prompt, 0, w, prm, None)
    y_sample, new_s = _trunk(x_sample, PAST_LEN, w, prm, past)
    return (y_prompt, y_sample) + new_p + new_s
```
